```python
import math
import jax, jax.numpy as jnp
from jax import lax
import numpy as np

D_MODEL = 4096
BATCH = 8
SEQ = 2048
DEPTH = 1

HEAD_DIM = 128
MOBA_HEADS = D_MODEL // (2 * HEAD_DIM)
DIFF_HEADS = D_MODEL // (4 * HEAD_DIM)
MOBA_WIDTH = MOBA_HEADS * HEAD_DIM
DIFF_QK_WIDTH = DIFF_HEADS * 2 * HEAD_DIM
DIFF_V_WIDTH = DIFF_HEADS * 2 * HEAD_DIM
MIX_WIDTH = MOBA_WIDTH + DIFF_V_WIDTH
IN_WIDTH = 3 * MOBA_WIDTH + 2 * DIFF_QK_WIDTH + DIFF_V_WIDTH
IN_SPLITS = (MOBA_WIDTH, 2 * MOBA_WIDTH, 3 * MOBA_WIDTH,
             3 * MOBA_WIDTH + DIFF_QK_WIDTH, 3 * MOBA_WIDTH + 2 * DIFF_QK_WIDTH)

MOBA_BLOCK = 256
MOBA_TOPK = 3
MOBA_Q_CHUNK = 8
DENSE_Q_BLOCK = 128
ROPE_THETA = 500000.0
ROT_DIM = HEAD_DIM // 4
MEM_LEN = 256
MEM_HEADS = 4
MEM_HEAD_DIM = 128
MEM_WIDTH = MEM_HEADS * MEM_HEAD_DIM
D_FF = -(-8 * D_MODEL // (3 * 256)) * 256
NORM_EPS = 1e-6

kernel_name = "hymba_moba_diffattn_hybrid_layer"


def rms_norm(x, g):
    xf = x.astype(jnp.float32)
    y = xf * lax.rsqrt(jnp.mean(xf * xf, axis=-1, keepdims=True) + NORM_EPS)
    return (y * g.astype(jnp.float32)).astype(x.dtype)


def rope_tables(seq):
    pos = jnp.arange(seq, dtype=jnp.float32)
    inv = ROPE_THETA ** (-jnp.arange(0, ROT_DIM, 2, dtype=jnp.float32) / ROT_DIM)
    ang = pos[:, None] * inv[None, :]
    return jnp.cos(ang), jnp.sin(ang)


def apply_partial_rope(x, cos, sin):
    half = ROT_DIM // 2
    x1 = x[..., :half].astype(jnp.float32)
    x2 = x[..., half:ROT_DIM].astype(jnp.float32)
    r1 = (x1 * cos - x2 * sin).astype(x.dtype)
    r2 = (x2 * cos + x1 * sin).astype(x.dtype)
    return jnp.concatenate([r1, r2, x[..., ROT_DIM:]], axis=-1)


def moba_attention(q, k, v):
    b, h, s, dh = q.shape
    nb = -(-s // MOBA_BLOCK)
    pad = nb * MOBA_BLOCK - s
    kp = jnp.pad(k, ((0, 0), (0, 0), (0, pad), (0, 0)))
    vp = jnp.pad(v, ((0, 0), (0, 0), (0, pad), (0, 0)))
    kb = kp.reshape(b, h, nb, MOBA_BLOCK, dh)
    vb = vp.reshape(b, h, nb, MOBA_BLOCK, dh)
    scale = dh ** -0.5
    n_sel = min(MOBA_TOPK, nb - 1)
    q_blk = jnp.arange(s) // MOBA_BLOCK
    if n_sel > 0:
        k_mean = jnp.mean(kb.astype(jnp.float32), axis=3)
        gate = jnp.einsum('bhsd,bhnd->bhsn', q.astype(jnp.float32), k_mean)
        past = jnp.arange(nb)[None, :] < q_blk[:, None]
        gate = jnp.where(past, gate, -jnp.inf)
        _, sel_idx = lax.top_k(gate, n_sel)
    bi = jnp.arange(b)[:, None, None, None]
    hi = jnp.arange(h)[None, :, None, None]
    C = MOBA_Q_CHUNK

    def chunk(c):
        start = c * C
        qc = lax.dynamic_slice_in_dim(q, start, C, axis=2)
        qpos = start + jnp.arange(C)
        own = start // MOBA_BLOCK
        k_own = lax.dynamic_index_in_dim(kb, own, axis=2, keepdims=False)
        v_own = lax.dynamic_index_in_dim(vb, own, axis=2, keepdims=False)
        kpos = own * MOBA_BLOCK + jnp.arange(MOBA_BLOCK)
        s_own = jnp.einsum('bhcd,bhnd->bhcn', qc, k_own).astype(jnp.float32) * scale
        s_own = jnp.where(kpos[None, :] <= qpos[:, None], s_own, -jnp.inf)
        if n_sel > 0:
            idx = lax.dynamic_slice_in_dim(sel_idx, start, C, axis=2)
            k_sel = kb[bi, hi, idx]
            v_sel = vb[bi, hi, idx]
            s_sel = jnp.einsum('bhcd,bhcjnd->bhcjn', qc, k_sel).astype(jnp.float32) * scale
            valid = idx < (qpos // MOBA_BLOCK)[None, None, :, None]
            s_sel = jnp.where(valid[..., None], s_sel, -jnp.inf)
            scores = jnp.concatenate(
                [s_sel.reshape(b, h, C, n_sel * MOBA_BLOCK), s_own], axis=-1)
            p = jax.nn.softmax(scores, axis=-1).astype(v.dtype)
            p_sel = p[..., :n_sel * MOBA_BLOCK].reshape(b, h, C, n_sel, MOBA_BLOCK)
            p_own = p[..., n_sel * MOBA_BLOCK:]
            return (jnp.einsum('bhcjn,bhcjnd->bhcd', p_sel, v_sel)
                    + jnp.einsum('bhcn,bhnd->bhcd', p_own, v_own))
        p_own = jax.nn.softmax(s_own, axis=-1).astype(v.dtype)
        return jnp.einsum('bhcn,bhnd->bhcd', p_own, v_own)

    outs = lax.map(chunk, jnp.arange(s // C))
    return outs.transpose(1, 2, 0, 3, 4).reshape(b, h, s, dh)


def diff_attention(q, k, v, lam, subln_g, lam_init):
    b, h, _, s, dh = q.shape
    scale = dh ** -0.5
    kpos = jnp.arange(s)
    QB = DENSE_Q_BLOCK

    def qblock(i):
        start = i * QB
        qb = lax.dynamic_slice_in_dim(q, start, QB, axis=3)
        sc = jnp.einsum('bhmqd,bhmkd->bhmqk', qb, k).astype(jnp.float32) * scale
        qpos = start + jnp.arange(QB)
        sc = jnp.where(kpos[None, :] <= qpos[:, None], sc, -jnp.inf)
        p = jax.nn.softmax(sc, axis=-1)
        w = p[:, :, 0] - lam * p[:, :, 1]
        return jnp.einsum('bhqk,bhke->bhqe', w.astype(v.dtype), v)

    o = lax.map(qblock, jnp.arange(s // QB))
    o = o.transpose(1, 2, 0, 3, 4).reshape(b, h, s, 2 * dh)
    return rms_norm(o, subln_g) * (1.0 - lam_init)


def memory_cross_attention(hn, mem_n, wq, wk, wv, wo, qn_g, kn_g):
    b, s, _ = hn.shape
    m = mem_n.shape[1]
    q = rms_norm((hn @ wq).reshape(b, s, MEM_HEADS, MEM_HEAD_DIM), qn_g)
    k = rms_norm((mem_n @ wk).reshape(b, m, MEM_HEADS, MEM_HEAD_DIM), kn_g)
    v = (mem_n @ wv).reshape(b, m, MEM_HEADS, MEM_HEAD_DIM)
    sc = jnp.einsum('bshd,bmhd->bhsm', q, k).astype(jnp.float32) * (MEM_HEAD_DIM ** -0.5)
    p = jax.nn.softmax(sc, axis=-1).astype(v.dtype)
    o = jnp.einsum('bhsm,bmhd->bshd', p, v).reshape(b, s, MEM_WIDTH)
    return o @ wo


def swiglu(hn, w_gate, w_up, w_down):
    return (jax.nn.silu(hn @ w_gate) * (hn @ w_up)) @ w_down


def setup_inputs(seed: int = 0) -> dict:
    key = jax.random.key(seed)
    ks = jax.random.split(key, 32)

    def nrm(k, shape, scale):
        return jax.random.normal(k, shape, jnp.float32) * scale

    def gain(k, shape):
        return 1.0 + 0.02 * jax.random.normal(k, shape, jnp.float32)

    L = DEPTH
    return {
        "x": nrm(ks[0], (BATCH, SEQ, D_MODEL), 1.0),
        "mem": nrm(ks[1], (BATCH, MEM_LEN, D_MODEL), 1.0),
        "norm_mix_g": gain(ks[2], (L, D_MODEL)),
        "w_in": nrm(ks[3], (L, D_MODEL, IN_WIDTH), D_MODEL ** -0.5),
        "q_norm_a": gain(ks[4], (L, HEAD_DIM)),
        "k_norm_a": gain(ks[5], (L, HEAD_DIM)),
        "q_norm_b": gain(ks[6], (L, HEAD_DIM)),
        "k_norm_b": gain(ks[7], (L, HEAD_DIM)),
        "lam_q1": nrm(ks[8], (L, HEAD_DIM), 0.1),
        "lam_k1": nrm(ks[9], (L, HEAD_DIM), 0.1),
        "lam_q2": nrm(ks[10], (L, HEAD_DIM), 0.1),
        "lam_k2": nrm(ks[11], (L, HEAD_DIM), 0.1),
        "diff_subln_g": gain(ks[12], (L, 2 * HEAD_DIM)),
        "w_out": nrm(ks[13], (L, MIX_WIDTH, D_MODEL), MIX_WIDTH ** -0.5),
        "norm_cross_g": gain(ks[14], (L, D_MODEL)),
        "norm_mem_g": gain(ks[15], (L, D_MODEL)),
        "w_mq": nrm(ks[16], (L, D_MODEL, MEM_WIDTH), D_MODEL ** -0.5),
        "w_mk": nrm(ks[17], (L, D_MODEL, MEM_WIDTH), D_MODEL ** -0.5),
        "w_mv": nrm(ks[18], (L, D_MODEL, MEM_WIDTH), D_MODEL ** -0.5),
        "w_mo": nrm(ks[19], (L, MEM_WIDTH, D_MODEL), MEM_WIDTH ** -0.5),
        "q_norm_m": gain(ks[20], (L, MEM_HEAD_DIM)),
        "k_norm_m": gain(ks[21], (L, MEM_HEAD_DIM)),
        "norm_ffn_g": gain(ks[22], (L, D_MODEL)),
        "w_gate": nrm(ks[23], (L, D_MODEL, D_FF), D_MODEL ** -0.5),
        "w_up": nrm(ks[24], (L, D_MODEL, D_FF), D_MODEL ** -0.5),
        "w_down": nrm(ks[25], (L, D_FF, D_MODEL), D_FF ** -0.5),
    }


def reference(x, mem, norm_mix_g, w_in, q_norm_a, k_norm_a, q_norm_b, k_norm_b,
              lam_q1, lam_k1, lam_q2, lam_k2, diff_subln_g, w_out,
              norm_cross_g, norm_mem_g, w_mq, w_mk, w_mv, w_mo, q_norm_m, k_norm_m,
              norm_ffn_g, w_gate, w_up, w_down):
    b, s, _ = x.shape
    cos, sin = rope_tables(s)
    h = x
    for l in range(DEPTH):
        lam_init = 0.8 - 0.6 * math.exp(-0.3 * l)
        xn = rms_norm(h, norm_mix_g[l])
        proj = xn @ w_in[l]
        qa, ka, va, qb, kb, vb = jnp.split(proj, IN_SPLITS, axis=-1)
        qa = qa.reshape(b, s, MOBA_HEADS, HEAD_DIM).transpose(0, 2, 1, 3)
        ka = ka.reshape(b, s, MOBA_HEADS, HEAD_DIM).transpose(0, 2, 1, 3)
        va = va.reshape(b, s, MOBA_HEADS, HEAD_DIM).transpose(0, 2, 1, 3)
        qa = apply_partial_rope(rms_norm(qa, q_norm_a[l]), cos, sin)
        ka = apply_partial_rope(rms_norm(ka, k_norm_a[l]), cos, sin)
        out_a = moba_attention(qa, ka, va)
        out_a = out_a.transpose(0, 2, 1, 3).reshape(b, s, MOBA_WIDTH)

        qb = qb.reshape(b, s, DIFF_HEADS, 2, HEAD_DIM).transpose(0, 2, 3, 1, 4)
        kb = kb.reshape(b, s, DIFF_HEADS, 2, HEAD_DIM).transpose(0, 2, 3, 1, 4)
        vb = vb.reshape(b, s, DIFF_HEADS, 2 * HEAD_DIM).transpose(0, 2, 1, 3)
        qb = apply_partial_rope(rms_norm(qb, q_norm_b[l]), cos, sin)
        kb = apply_partial_rope(rms_norm(kb, k_norm_b[l]), cos, sin)
        lam = (jnp.exp(jnp.sum(lam_q1[l].astype(jnp.float32) * lam_k1[l].astype(jnp.float32)))
               - jnp.exp(jnp.sum(lam_q2[l].astype(jnp.float32) * lam_k2[l].astype(jnp.float32)))
               + lam_init)
        out_b = diff_attention(qb, kb, vb, lam, diff_subln_g[l], lam_init)
        out_b = out_b.transpose(0, 2, 1, 3).reshape(b, s, DIFF_V_WIDTH)

        h = h + jnp.concatenate([out_a, out_b], axis=-1) @ w_out[l]
        hn = rms_norm(h, norm_cross_g[l])
        mem_n = rms_norm(mem, norm_mem_g[l])
        h = h + memory_cross_attention(hn, mem_n, w_mq[l], w_mk[l], w_mv[l], w_mo[l],
                                       q_norm_m[l], k_norm_m[l])
        h = h + swiglu(rms_norm(h, norm_ffn_g[l]), w_gate[l], w_up[l], w_down[l])
    return h
```

```python
import functools
import math

import jax
import jax.numpy as jnp
from jax import lax
from jax.experimental import pallas as pl
from jax.experimental.pallas import tpu as pltpu

F32 = jnp.float32
BF16 = jnp.bfloat16

D_MODEL = 4096
HEAD_DIM = 128
MOBA_HEADS = 16
DIFF_HEADS = 8
MOBA_WIDTH = MOBA_HEADS * HEAD_DIM
DIFF_V_DIM = 2 * HEAD_DIM
IN_WIDTH = 12288
MOBA_BLOCK = 256
MOBA_TOPK = 3
ROPE_THETA = 500000.0
ROT_DIM = HEAD_DIM // 4
MEM_HEADS = 4
MEM_WIDTH = MEM_HEADS * HEAD_DIM
NORM_EPS = 1e-6
ATTN_SCALE = HEAD_DIM ** -0.5
MASKED = -1e30

V7X_VMEM_BYTES = 64 * 1024 * 1024
VMEM_LIMIT = V7X_VMEM_BYTES - 8 * 1024 * 1024

NT_DIMS = (((1,), (1,)), ((), ()))


def _params(n_grid):
    return pltpu.CompilerParams(
        dimension_semantics=("parallel",) * (n_grid - 1) + ("arbitrary",),
        vmem_limit_bytes=VMEM_LIMIT,
    )


def _rmsnorm_kernel(x_ref, g_ref, o_ref):
    x = x_ref[...].astype(F32)
    ms = jnp.mean(x * x, axis=-1, keepdims=True)
    o_ref[...] = (x * lax.rsqrt(ms + NORM_EPS) * g_ref[...]).astype(o_ref.dtype)


def _rmsnorm(x, g, tm):
    m, d = x.shape
    return pl.pallas_call(
        _rmsnorm_kernel,
        grid=(m // tm,),
        in_specs=[pl.BlockSpec((tm, d), lambda i: (i, 0)), pl.BlockSpec((1, d), lambda i: (0, 0))],
        out_specs=pl.BlockSpec((tm, d), lambda i: (i, 0)),
        out_shape=jax.ShapeDtypeStruct((m, d), BF16),
        compiler_params=_params(1),
        name="rmsnorm",
    )(x, g.reshape(1, d).astype(F32))


def _mm_kernel(x_ref, w_ref, o_ref):
    o_ref[...] = jnp.dot(x_ref[...], w_ref[...], preferred_element_type=F32).astype(o_ref.dtype)


def _matmul(x, w, tm, tn, out_dtype):
    m, k = x.shape
    n = w.shape[1]
    return pl.pallas_call(
        _mm_kernel,
        grid=(m // tm, n // tn),
        in_specs=[pl.BlockSpec((tm, k), lambda i, j: (i, 0)), pl.BlockSpec((k, tn), lambda i, j: (0, j))],
        out_specs=pl.BlockSpec((tm, tn), lambda i, j: (i, j)),
        out_shape=jax.ShapeDtypeStruct((m, n), out_dtype),
        compiler_params=_params(2),
        name="matmul",
    )(x, w)


def _mm_res_kernel(x_ref, w_ref, r_ref, o_ref):
    o_ref[...] = r_ref[...] + jnp.dot(x_ref[...], w_ref[...], preferred_element_type=F32)


def _matmul_residual(x, w, r, tm, tn):
    m, k = x.shape
    n = w.shape[1]
    return pl.pallas_call(
        _mm_res_kernel,
        grid=(m // tm, n // tn),
        in_specs=[
            pl.BlockSpec((tm, k), lambda i, j: (i, 0)),
            pl.BlockSpec((k, tn), lambda i, j: (0, j)),
            pl.BlockSpec((tm, tn), lambda i, j: (i, j)),
        ],
        out_specs=pl.BlockSpec((tm, tn), lambda i, j: (i, j)),
        out_shape=jax.ShapeDtypeStruct((m, n), F32),
        compiler_params=_params(2),
        name="matmul_residual",
    )(x, w, r)


def _mm2_res_kernel(a_ref, b_ref, wa_ref, wb_ref, r_ref, o_ref):
    acc = jnp.dot(a_ref[...], wa_ref[...], preferred_element_type=F32)
    acc = acc + jnp.dot(b_ref[...], wb_ref[...], preferred_element_type=F32)
    o_ref[...] = r_ref[...] + acc


def _out_projection(a, b, w, r, tm, tn):
    m, ka = a.shape
    kb = b.shape[1]
    assert ka == kb and w.shape[0] == ka + kb
    n = w.shape[1]
    return pl.pallas_call(
        _mm2_res_kernel,
        grid=(m // tm, n // tn),
        in_specs=[
            pl.BlockSpec((tm, ka), lambda i, j: (i, 0)),
            pl.BlockSpec((tm, kb), lambda i, j: (i, 0)),
            pl.BlockSpec((ka, tn), lambda i, j: (0, j)),
            pl.BlockSpec((kb, tn), lambda i, j: (1, j)),
            pl.BlockSpec((tm, tn), lambda i, j: (i, j)),
        ],
        out_specs=pl.BlockSpec((tm, tn), lambda i, j: (i, j)),
        out_shape=jax.ShapeDtypeStruct((m, n), F32),
        compiler_params=_params(2),
        name="out_projection",
    )(a, b, w, w, r)


def _swiglu_kernel(x_ref, wg_ref, wu_ref, o_ref):
    x = x_ref[...]
    g = jnp.dot(x, wg_ref[...], preferred_element_type=F32)
    u = jnp.dot(x, wu_ref[...], preferred_element_type=F32)
    o_ref[...] = (jax.nn.silu(g) * u).astype(o_ref.dtype)


def _swiglu_up(x, wg, wu, tm, tn):
    m, k = x.shape
    n = wg.shape[1]
    return pl.pallas_call(
        _swiglu_kernel,
        grid=(m // tm, n // tn),
        in_specs=[
            pl.BlockSpec((tm, k), lambda i, j: (i, 0)),
            pl.BlockSpec((k, tn), lambda i, j: (0, j)),
            pl.BlockSpec((k, tn), lambda i, j: (0, j)),
        ],
        out_specs=pl.BlockSpec((tm, tn), lambda i, j: (i, j)),
        out_shape=jax.ShapeDtypeStruct((m, n), BF16),
        compiler_params=_params(2),
        name="swiglu_up",
    )(x, wg, wu)


def _rope_tables(seq):
    half = ROT_DIM // 2
    pos = jnp.arange(seq, dtype=F32)
    inv = ROPE_THETA ** (-jnp.arange(0, ROT_DIM, 2, dtype=F32) / ROT_DIM)
    ang = pos[:, None] * inv[None, :]
    cos, sin = jnp.cos(ang), jnp.sin(ang)
    rest = HEAD_DIM - ROT_DIM
    c = jnp.concatenate([cos, cos, jnp.ones((seq, rest), F32)], axis=-1)
    sa = jnp.concatenate([-sin, jnp.zeros((seq, HEAD_DIM - half), F32)], axis=-1)
    sb = jnp.concatenate([jnp.zeros((seq, half), F32), sin, jnp.zeros((seq, rest), F32)], axis=-1)
    return c, sa, sb


def _norm_rope(x, g, c, sa, sb):
    half = ROT_DIM // 2
    ms = jnp.mean(x * x, axis=-1, keepdims=True)
    y = x * lax.rsqrt(ms + NORM_EPS) * g
    up = pltpu.roll(y, HEAD_DIM - half, 1)
    down = pltpu.roll(y, half, 1)
    return y * c + up * sa + down * sb


def _softmax_block(s, m_prev, l_prev):
    m_new = jnp.maximum(m_prev, jnp.max(s, axis=0, keepdims=True))
    alpha = jnp.exp(m_prev - m_new)
    p = jnp.exp(s - m_new)
    l_new = alpha * l_prev + jnp.sum(p, axis=0, keepdims=True)
    return p.astype(BF16), m_new, l_new, alpha


def _causal_mask(s):
    key = lax.broadcasted_iota(jnp.int32, s.shape, 0)
    qry = lax.broadcasted_iota(jnp.int32, s.shape, 1)
    return jnp.where(key <= qry, s, MASKED)


def _moba_kernel(q_ref, k_ref, v_ref, gq_ref, gk_ref, c_ref, sa_ref, sb_ref, o_ref,
                 kn_ref, vt_ref, kmean_ref, bias_ref):
    blk = MOBA_BLOCK
    nb = q_ref.shape[1] // blk

    for n in range(nb):
        rows = slice(n * blk, (n + 1) * blk)
        kn = _norm_rope(k_ref[0, rows, :].astype(F32), gk_ref[...],
                        c_ref[rows, :], sa_ref[rows, :], sb_ref[rows, :])
        kn_ref[n] = kn.astype(BF16)
        kmean_ref[n:n + 1, :] = jnp.mean(kn, axis=0, keepdims=True)
        vt_ref[n] = v_ref[0, rows, :].astype(F32).T.astype(BF16)

    for qi in range(nb):
        rows = slice(qi * blk, (qi + 1) * blk)
        qn = _norm_rope(q_ref[0, rows, :].astype(F32), gq_ref[...],
                        c_ref[rows, :], sa_ref[rows, :], sb_ref[rows, :])
        qb = qn.astype(BF16)

        s = lax.dot_general(kn_ref[qi], qb, NT_DIMS, preferred_element_type=F32) * ATTN_SCALE
        s = _causal_mask(s)
        m = jnp.max(s, axis=0, keepdims=True)
        p = jnp.exp(s - m)
        l = jnp.sum(p, axis=0, keepdims=True)
        acc = jnp.dot(vt_ref[qi], p.astype(BF16), preferred_element_type=F32)

        if qi > 0:
            if qi > MOBA_TOPK:
                gate = lax.dot_general(kmean_ref[...], qn, NT_DIMS, preferred_element_type=F32,
                                       precision=lax.Precision.HIGHEST)
                n_iota = lax.broadcasted_iota(jnp.int32, gate.shape, 0)
                beaten = jnp.zeros(gate.shape, jnp.int32)
                for mth in range(qi):
                    gm = gate[mth:mth + 1, :]
                    wins = (gm > gate) | ((gm == gate) & (mth < n_iota))
                    beaten = beaten + wins.astype(jnp.int32)
                bias = jnp.where(beaten < MOBA_TOPK, 0.0, MASKED)
                for n in range(qi):
                    bias_ref[n] = bias[n:n + 1, :]

            def past_block(j, carry, masked=qi > MOBA_TOPK):
                m_prev, l_prev, acc_prev = carry
                sj = lax.dot_general(kn_ref[j], qb, NT_DIMS, preferred_element_type=F32) * ATTN_SCALE
                if masked:
                    sj = sj + bias_ref[j]
                pj, m_new, l_new, alpha = _softmax_block(sj, m_prev, l_prev)
                acc_new = acc_prev * alpha + jnp.dot(vt_ref[j], pj, preferred_element_type=F32)
                return m_new, l_new, acc_new

            m, l, acc = lax.fori_loop(0, qi, past_block, (m, l, acc))

        o_ref[0, rows, :] = (acc / l).T.astype(o_ref.dtype)


def _moba_attention(proj, gq, gk, tables):
    b, s, _ = proj.shape
    nb = s // MOBA_BLOCK
    c, sa, sb = tables
    head = lambda off: pl.BlockSpec((1, s, HEAD_DIM), lambda bi, h: (bi, 0, off + h))
    full = lambda shape: pl.BlockSpec(shape, lambda bi, h: (0,) * len(shape))
    return pl.pallas_call(
        _moba_kernel,
        grid=(b, MOBA_HEADS),
        in_specs=[head(0), head(MOBA_HEADS), head(2 * MOBA_HEADS),
                  full((1, HEAD_DIM)), full((1, HEAD_DIM)),
                  full((s, HEAD_DIM)), full((s, HEAD_DIM)), full((s, HEAD_DIM))],
        out_specs=pl.BlockSpec((1, s, HEAD_DIM), lambda bi, h: (bi, 0, h)),
        out_shape=jax.ShapeDtypeStruct((b, s, MOBA_WIDTH), BF16),
        scratch_shapes=[
            pltpu.VMEM((nb, MOBA_BLOCK, HEAD_DIM), BF16),
            pltpu.VMEM((nb, HEAD_DIM, MOBA_BLOCK), BF16),
            pltpu.VMEM((nb, HEAD_DIM), F32),
            pltpu.VMEM((nb, 1, MOBA_BLOCK), F32),
        ],
        compiler_params=_params(2),
        name="moba_attention",
    )(proj, proj, proj, gq.reshape(1, -1).astype(F32), gk.reshape(1, -1).astype(F32), c, sa, sb)


def _diff_kernel(lam_init, q_ref, k_ref, v_ref, gq_ref, gk_ref, lq1_ref, lk1_ref, lq2_ref, lk2_ref,
                 gs_ref, c_ref, sa_ref, sb_ref, o_ref, k1_ref, k2_ref, vt_ref, acc1_ref, acc2_ref):
    blk = MOBA_BLOCK
    nb = q_ref.shape[1] // blk
    hd = HEAD_DIM

    lam = (jnp.exp(jnp.sum(lq1_ref[...] * lk1_ref[...], axis=-1, keepdims=True))
           - jnp.exp(jnp.sum(lq2_ref[...] * lk2_ref[...], axis=-1, keepdims=True))
           + lam_init)

    def halves(ref, rows, g):
        tabs = (c_ref[rows, :], sa_ref[rows, :], sb_ref[rows, :])
        x1 = _norm_rope(ref[0, rows, 0:hd].astype(F32), g, *tabs)
        x2 = _norm_rope(ref[0, rows, hd:2 * hd].astype(F32), g, *tabs)
        return x1.astype(BF16), x2.astype(BF16)

    for n in range(nb):
        rows = slice(n * blk, (n + 1) * blk)
        k1_ref[n], k2_ref[n] = halves(k_ref, rows, gk_ref[...])
        vt_ref[n] = v_ref[0, rows, :].astype(F32).T.astype(BF16)

    for qi in range(nb):
        rows = slice(qi * blk, (qi + 1) * blk)
        q1, q2 = halves(q_ref, rows, gq_ref[...])

        stats = []
        for kref, qh, acc_ref in ((k1_ref, q1, acc1_ref), (k2_ref, q2, acc2_ref)):
            s = lax.dot_general(kref[qi], qh, NT_DIMS, preferred_element_type=F32) * ATTN_SCALE
            s = _causal_mask(s)
            m = jnp.max(s, axis=0, keepdims=True)
            p = jnp.exp(s - m)
            l = jnp.sum(p, axis=0, keepdims=True)
            acc_ref[...] = jnp.dot(vt_ref[qi], p.astype(BF16), preferred_element_type=F32)
            stats += [m, l]

        if qi > 0:
            def past_block(j, carry):
                out = []
                for idx, (kref, qh, acc_ref) in enumerate(((k1_ref, q1, acc1_ref), (k2_ref, q2, acc2_ref))):
                    m_prev, l_prev = carry[2 * idx], carry[2 * idx + 1]
                    sj = lax.dot_general(kref[j], qh, NT_DIMS, preferred_element_type=F32) * ATTN_SCALE
                    pj, m_new, l_new, alpha = _softmax_block(sj, m_prev, l_prev)
                    acc_ref[...] = acc_ref[...] * alpha + jnp.dot(vt_ref[j], pj, preferred_element_type=F32)
                    out += [m_new, l_new]
                return tuple(out)

            stats = lax.fori_loop(0, qi, past_block, tuple(stats))

        _, l1, _, l2 = stats
        o = acc1_ref[...] / l1 - lam * (acc2_ref[...] / l2)
        ms = jnp.mean(o * o, axis=0, keepdims=True)
        on = (o * lax.rsqrt(ms + NORM_EPS)).T * gs_ref[...]
        o_ref[0, rows, :] = (on * (1.0 - lam_init)).astype(o_ref.dtype)


def _diff_attention(proj, gq, gk, lq1, lk1, lq2, lk2, gs, lam_init, tables):
    b, s, _ = proj.shape
    nb = s // MOBA_BLOCK
    c, sa, sb = tables
    qoff = 3 * MOBA_WIDTH // DIFF_V_DIM
    head = lambda off: pl.BlockSpec((1, s, DIFF_V_DIM), lambda bi, h: (bi, 0, off + h))
    full = lambda shape: pl.BlockSpec(shape, lambda bi, h: (0,) * len(shape))
    row = lambda v: v.reshape(1, -1).astype(F32)
    return pl.pallas_call(
        functools.partial(_diff_kernel, lam_init),
        grid=(b, DIFF_HEADS),
        in_specs=[head(qoff), head(qoff + DIFF_HEADS), head(qoff + 2 * DIFF_HEADS),
                  full((1, HEAD_DIM)), full((1, HEAD_DIM)),
                  full((1, HEAD_DIM)), full((1, HEAD_DIM)), full((1, HEAD_DIM)), full((1, HEAD_DIM)),
                  full((1, DIFF_V_DIM)),
                  full((s, HEAD_DIM)), full((s, HEAD_DIM)), full((s, HEAD_DIM))],
        out_specs=pl.BlockSpec((1, s, DIFF_V_DIM), lambda bi, h: (bi, 0, h)),
        out_shape=jax.ShapeDtypeStruct((b, s, DIFF_HEADS * DIFF_V_DIM), BF16),
        scratch_shapes=[
            pltpu.VMEM((nb, MOBA_BLOCK, HEAD_DIM), BF16),
            pltpu.VMEM((nb, MOBA_BLOCK, HEAD_DIM), BF16),
            pltpu.VMEM((nb, DIFF_V_DIM, MOBA_BLOCK), BF16),
            pltpu.VMEM((DIFF_V_DIM, MOBA_BLOCK), F32),
            pltpu.VMEM((DIFF_V_DIM, MOBA_BLOCK), F32),
        ],
        compiler_params=_params(2),
        name="diff_attention",
    )(proj, proj, proj, row(gq), row(gk), row(lq1), row(lk1), row(lq2), row(lk2), row(gs), c, sa, sb)


def _mem_kv_kernel(m_ref, wk_ref, wv_ref, gk_ref, k_ref, v_ref):
    x = m_ref[...]
    k = jnp.dot(x, wk_ref[...], preferred_element_type=F32)
    for h in range(MEM_HEADS):
        cols = slice(h * HEAD_DIM, (h + 1) * HEAD_DIM)
        kh = k[:, cols]
        ms = jnp.mean(kh * kh, axis=-1, keepdims=True)
        k_ref[:, cols] = (kh * lax.rsqrt(ms + NORM_EPS) * gk_ref[...]).astype(k_ref.dtype)
    v_ref[...] = jnp.dot(x, wv_ref[...], preferred_element_type=F32).astype(v_ref.dtype)


def _mem_kv(mem_n, wk, wv, gk, tm):
    m, d = mem_n.shape
    return pl.pallas_call(
        _mem_kv_kernel,
        grid=(m // tm,),
        in_specs=[pl.BlockSpec((tm, d), lambda i: (i, 0)),
                  pl.BlockSpec((d, MEM_WIDTH), lambda i: (0, 0)),
                  pl.BlockSpec((d, MEM_WIDTH), lambda i: (0, 0)),
                  pl.BlockSpec((1, HEAD_DIM), lambda i: (0, 0))],
        out_specs=[pl.BlockSpec((tm, MEM_WIDTH), lambda i: (i, 0)),
                   pl.BlockSpec((tm, MEM_WIDTH), lambda i: (i, 0))],
        out_shape=[jax.ShapeDtypeStruct((m, MEM_WIDTH), BF16), jax.ShapeDtypeStruct((m, MEM_WIDTH), BF16)],
        compiler_params=_params(1),
        name="mem_kv",
    )(mem_n, wk, wv, gk.reshape(1, -1).astype(F32))


def _cross_kernel(hn_ref, h_ref, k_ref, v_ref, wq_ref, wo_ref, gq_ref, o_ref):
    q = jnp.dot(hn_ref[...], wq_ref[...], preferred_element_type=F32)
    heads = []
    for h in range(MEM_HEADS):
        cols = slice(h * HEAD_DIM, (h + 1) * HEAD_DIM)
        qh = q[:, cols]
        ms = jnp.mean(qh * qh, axis=-1, keepdims=True)
        qh = (qh * lax.rsqrt(ms + NORM_EPS) * gq_ref[...]).astype(BF16)
        s = lax.dot_general(qh, k_ref[0, :, cols], NT_DIMS, preferred_element_type=F32) * ATTN_SCALE
        e = jnp.exp(s - jnp.max(s, axis=-1, keepdims=True))
        p = (e / jnp.sum(e, axis=-1, keepdims=True)).astype(BF16)
        heads.append(jnp.dot(p, v_ref[0, :, cols], preferred_element_type=F32).astype(BF16))
    o = jnp.concatenate(heads, axis=-1)
    o_ref[...] = h_ref[...] + jnp.dot(o, wo_ref[...], preferred_element_type=F32)


def _cross_attention(hn, h, k, v, wq, wo, gq, seq, tm):
    m, d = hn.shape
    mem_len = k.shape[1]
    per_batch = seq // tm
    return pl.pallas_call(
        _cross_kernel,
        grid=(m // tm,),
        in_specs=[pl.BlockSpec((tm, d), lambda i: (i, 0)),
                  pl.BlockSpec((tm, d), lambda i: (i, 0)),
                  pl.BlockSpec((1, mem_len, MEM_WIDTH), lambda i: (i // per_batch, 0, 0)),
                  pl.BlockSpec((1, mem_len, MEM_WIDTH), lambda i: (i // per_batch, 0, 0)),
                  pl.BlockSpec((d, MEM_WIDTH), lambda i: (0, 0)),
                  pl.BlockSpec((MEM_WIDTH, d), lambda i: (0, 0)),
                  pl.BlockSpec((1, HEAD_DIM), lambda i: (0, 0))],
        out_specs=pl.BlockSpec((tm, d), lambda i: (i, 0)),
        out_shape=jax.ShapeDtypeStruct((m, d), F32),
        compiler_params=_params(1),
        name="cross_attention",
    )(hn, h, k, v, wq, wo, gq.reshape(1, -1).astype(F32))


def _layer(h, mem_n_src, l, p):
    b, s, d = h.shape
    m = b * s
    lam_init = 0.8 - 0.6 * math.exp(-0.3 * l)
    tables = _rope_tables(s)
    bf = lambda w: w.astype(BF16)
    h2d = h.reshape(m, d)

    xn = _rmsnorm(h2d, p["norm_mix_g"][l], 256)
    proj = _matmul(xn, bf(p["w_in"][l]), 1024, 1024, BF16).reshape(b, s, IN_WIDTH)
    out_a = _moba_attention(proj, p["q_norm_a"][l], p["k_norm_a"][l], tables)
    out_b = _diff_attention(proj, p["q_norm_b"][l], p["k_norm_b"][l], p["lam_q1"][l], p["lam_k1"][l],
                            p["lam_q2"][l], p["lam_k2"][l], p["diff_subln_g"][l], lam_init, tables)
    h2d = _out_projection(out_a.reshape(m, -1), out_b.reshape(m, -1), bf(p["w_out"][l]), h2d, 1024, 512)

    hn = _rmsnorm(h2d, p["norm_cross_g"][l], 256)
    mem_len = mem_n_src.shape[1]
    mem_n = _rmsnorm(mem_n_src.reshape(b * mem_len, d), p["norm_mem_g"][l], 256)
    mk, mv = _mem_kv(mem_n, bf(p["w_mk"][l]), bf(p["w_mv"][l]), p["k_norm_m"][l], 256)
    h2d = _cross_attention(hn, h2d, mk.reshape(b, mem_len, -1), mv.reshape(b, mem_len, -1),
                           bf(p["w_mq"][l]), bf(p["w_mo"][l]), p["q_norm_m"][l], s, 256)

    hn = _rmsnorm(h2d, p["norm_ffn_g"][l], 256)
    act = _swiglu_up(hn, bf(p["w_gate"][l]), bf(p["w_up"][l]), 1024, 256)
    h2d = _matmul_residual(act, bf(p["w_down"][l]), h2d, 512, 512)
    return h2d.reshape(b, s, d)


def kernel(x, mem, norm_mix_g, w_in, q_norm_a, k_norm_a, q_norm_b, k_norm_b, lam_q1, lam_k1, lam_q2, lam_k2, diff_subln_g, w_out, norm_cross_g, norm_mem_g, w_mq, w_mk, w_mv, w_mo, q_norm_m, k_norm_m, norm_ffn_g, w_gate, w_up, w_down):
    p = dict(norm_mix_g=norm_mix_g, w_in=w_in, q_norm_a=q_norm_a, k_norm_a=k_norm_a, q_norm_b=q_norm_b,
             k_norm_b=k_norm_b, lam_q1=lam_q1, lam_k1=lam_k1, lam_q2=lam_q2, lam_k2=lam_k2,
             diff_subln_g=diff_subln_g, w_out=w_out, norm_cross_g=norm_cross_g, norm_mem_g=norm_mem_g,
             w_mq=w_mq, w_mk=w_mk, w_mv=w_mv, w_mo=w_mo, q_norm_m=q_norm_m, k_norm_m=k_norm_m,
             norm_ffn_g=norm_ffn_g, w_gate=w_gate, w_up=w_up, w_down=w_down)
    h = x
    for l in range(w_in.shape[0]):
        h = _layer(h, mem, l, p)
    return h
```

```python
import functools
import math

import jax
import jax.numpy as jnp
from jax import lax
from jax.experimental import pallas as pl
from jax.experimental.pallas import tpu as pltpu

F32 = jnp.float32
BF16 = jnp.bfloat16

D_MODEL = 4096
HEAD_DIM = 128
MOBA_HEADS = 16
DIFF_HEADS = 8
MOBA_WIDTH = MOBA_HEADS * HEAD_DIM
DIFF_V_DIM = 2 * HEAD_DIM
IN_WIDTH = 12288
MOBA_BLOCK = 256
MOBA_TOPK = 3
ROPE_THETA = 500000.0
ROT_DIM = HEAD_DIM // 4
MEM_HEADS = 4
MEM_WIDTH = MEM_HEADS * HEAD_DIM
NORM_EPS = 1e-6
ATTN_SCALE = HEAD_DIM ** -0.5
MASKED = -1e30
EXP2_SCALE = ATTN_SCALE * math.log2(math.e)

V7X_VMEM_BYTES = 64 * 1024 * 1024
VMEM_LIMIT = V7X_VMEM_BYTES - 8 * 1024 * 1024

NT_DIMS = (((1,), (1,)), ((), ()))


def _params(n_grid):
    return pltpu.CompilerParams(
        dimension_semantics=("parallel",) * (n_grid - 1) + ("arbitrary",),
        vmem_limit_bytes=VMEM_LIMIT,
    )


def _rmsnorm_kernel(x_ref, g_ref, o_ref):
    x = x_ref[...].astype(F32)
    ms = jnp.mean(x * x, axis=-1, keepdims=True)
    o_ref[...] = (x * lax.rsqrt(ms + NORM_EPS) * g_ref[...]).astype(o_ref.dtype)


def _rmsnorm(x, g, tm):
    m, d = x.shape
    return pl.pallas_call(
        _rmsnorm_kernel,
        grid=(m // tm,),
        in_specs=[pl.BlockSpec((tm, d), lambda i: (i, 0)), pl.BlockSpec((1, d), lambda i: (0, 0))],
        out_specs=pl.BlockSpec((tm, d), lambda i: (i, 0)),
        out_shape=jax.ShapeDtypeStruct((m, d), BF16),
        compiler_params=_params(1),
        name="rmsnorm",
    )(x, g.reshape(1, d).astype(F32))


def _mm_kernel(x_ref, w_ref, o_ref):
    o_ref[...] = jnp.dot(x_ref[...], w_ref[...], preferred_element_type=F32).astype(o_ref.dtype)


def _matmul(x, w, tm, tn, out_dtype):
    m, k = x.shape
    n = w.shape[1]
    return pl.pallas_call(
        _mm_kernel,
        grid=(m // tm, n // tn),
        in_specs=[pl.BlockSpec((tm, k), lambda i, j: (i, 0)), pl.BlockSpec((k, tn), lambda i, j: (0, j))],
        out_specs=pl.BlockSpec((tm, tn), lambda i, j: (i, j)),
        out_shape=jax.ShapeDtypeStruct((m, n), out_dtype),
        compiler_params=_params(2),
        name="matmul",
    )(x, w)


def _mm_res_kernel(x_ref, w_ref, r_ref, o_ref):
    o_ref[...] = r_ref[...] + jnp.dot(x_ref[...], w_ref[...], preferred_element_type=F32)


def _matmul_residual(x, w, r, tm, tn):
    m, k = x.shape
    n = w.shape[1]
    return pl.pallas_call(
        _mm_res_kernel,
        grid=(m // tm, n // tn),
        in_specs=[
            pl.BlockSpec((tm, k), lambda i, j: (i, 0)),
            pl.BlockSpec((k, tn), lambda i, j: (0, j)),
            pl.BlockSpec((tm, tn), lambda i, j: (i, j)),
        ],
        out_specs=pl.BlockSpec((tm, tn), lambda i, j: (i, j)),
        out_shape=jax.ShapeDtypeStruct((m, n), F32),
        compiler_params=_params(2),
        name="matmul_residual",
    )(x, w, r)


def _mm2_res_kernel(a_ref, b_ref, wa_ref, wb_ref, r_ref, o_ref):
    acc = jnp.dot(a_ref[...], wa_ref[...], preferred_element_type=F32)
    acc = acc + jnp.dot(b_ref[...], wb_ref[...], preferred_element_type=F32)
    o_ref[...] = r_ref[...] + acc


def _out_projection(a, b, w, r, tm, tn):
    m, ka = a.shape
    kb = b.shape[1]
    assert ka == kb and w.shape[0] == ka + kb
    n = w.shape[1]
    return pl.pallas_call(
        _mm2_res_kernel,
        grid=(m // tm, n // tn),
        in_specs=[
            pl.BlockSpec((tm, ka), lambda i, j: (i, 0)),
            pl.BlockSpec((tm, kb), lambda i, j: (i, 0)),
            pl.BlockSpec((ka, tn), lambda i, j: (0, j)),
            pl.BlockSpec((kb, tn), lambda i, j: (1, j)),
            pl.BlockSpec((tm, tn), lambda i, j: (i, j)),
        ],
        out_specs=pl.BlockSpec((tm, tn), lambda i, j: (i, j)),
        out_shape=jax.ShapeDtypeStruct((m, n), F32),
        compiler_params=_params(2),
        name="out_projection",
    )(a, b, w, w, r)


def _swiglu_kernel(x_ref, wg_ref, wu_ref, o_ref):
    x = x_ref[...]
    g = jnp.dot(x, wg_ref[...], preferred_element_type=F32)
    u = jnp.dot(x, wu_ref[...], preferred_element_type=F32)
    o_ref[...] = (jax.nn.silu(g) * u).astype(o_ref.dtype)


def _swiglu_up(x, wg, wu, tm, tn):
    m, k = x.shape
    n = wg.shape[1]
    return pl.pallas_call(
        _swiglu_kernel,
        grid=(m // tm, n // tn),
        in_specs=[
            pl.BlockSpec((tm, k), lambda i, j: (i, 0)),
            pl.BlockSpec((k, tn), lambda i, j: (0, j)),
            pl.BlockSpec((k, tn), lambda i, j: (0, j)),
        ],
        out_specs=pl.BlockSpec((tm, tn), lambda i, j: (i, j)),
        out_shape=jax.ShapeDtypeStruct((m, n), BF16),
        compiler_params=_params(2),
        name="swiglu_up",
    )(x, wg, wu)


def _rope_tables(seq):
    half = ROT_DIM // 2
    pos = jnp.arange(seq, dtype=F32)
    inv = ROPE_THETA ** (-jnp.arange(0, ROT_DIM, 2, dtype=F32) / ROT_DIM)
    ang = pos[:, None] * inv[None, :]
    cos, sin = jnp.cos(ang), jnp.sin(ang)
    rest = HEAD_DIM - ROT_DIM
    c = jnp.concatenate([cos, cos, jnp.ones((seq, rest), F32)], axis=-1)
    sa = jnp.concatenate([-sin, jnp.zeros((seq, HEAD_DIM - half), F32)], axis=-1)
    sb = jnp.concatenate([jnp.zeros((seq, half), F32), sin, jnp.zeros((seq, rest), F32)], axis=-1)
    return c, sa, sb


def _norm_rope(x, g, c, sa, sb):
    half = ROT_DIM // 2
    ms = jnp.mean(x * x, axis=-1, keepdims=True)
    y = x * lax.rsqrt(ms + NORM_EPS) * g
    up = pltpu.roll(y, HEAD_DIM - half, 1)
    down = pltpu.roll(y, half, 1)
    return y * c + up * sa + down * sb


def _causal_mask(s):
    key = lax.broadcasted_iota(jnp.int32, s.shape, 0)
    qry = lax.broadcasted_iota(jnp.int32, s.shape, 1)
    return jnp.where(key <= qry, s, MASKED)


def _prefix_softmax(blocks):
    mx = blocks[0]
    for sj in blocks[1:]:
        mx = jnp.maximum(mx, sj)
    m = jnp.max(mx, axis=0, keepdims=True)
    ps = [jnp.exp2((sj - m) * EXP2_SCALE) for sj in blocks]
    tot = ps[0]
    for pj in ps[1:]:
        tot = tot + pj
    return ps, jnp.sum(tot, axis=0, keepdims=True)


def _split_blocks(s, n):
    return [s[j * MOBA_BLOCK:(j + 1) * MOBA_BLOCK, :] for j in range(n)]


def _moba_kernel(q_ref, k_ref, v_ref, gq_ref, gk_ref, c_ref, sa_ref, sb_ref, o_ref,
                 kn_ref, vt_ref, kmean_ref):
    blk = MOBA_BLOCK
    nb = q_ref.shape[1] // blk

    for n in range(nb):
        rows = slice(n * blk, (n + 1) * blk)
        kn = _norm_rope(k_ref[0, rows, :].astype(F32), gk_ref[...],
                        c_ref[rows, :], sa_ref[rows, :], sb_ref[rows, :])
        kn_ref[rows, :] = kn.astype(BF16)
        kmean_ref[n:n + 1, :] = jnp.mean(kn, axis=0, keepdims=True)
        vt_ref[:, rows] = v_ref[0, rows, :].astype(F32).T.astype(BF16)

    for qi in range(nb):
        rows = slice(qi * blk, (qi + 1) * blk)
        nk = (qi + 1) * blk
        qn = _norm_rope(q_ref[0, rows, :].astype(F32), gq_ref[...],
                        c_ref[rows, :], sa_ref[rows, :], sb_ref[rows, :])
        s = lax.dot_general(kn_ref[0:nk, :], qn.astype(BF16), NT_DIMS, preferred_element_type=F32)
        blocks = _split_blocks(s, qi + 1)
        blocks[qi] = _causal_mask(blocks[qi])

        if qi > MOBA_TOPK:
            gate = lax.dot_general(kmean_ref[...], qn, NT_DIMS, preferred_element_type=F32,
                                   precision=lax.Precision.HIGHEST)
            n_iota = lax.broadcasted_iota(jnp.int32, gate.shape, 0)
            beaten = jnp.zeros(gate.shape, jnp.int32)
            for mth in range(qi):
                gm = gate[mth:mth + 1, :]
                wins = (gm > gate) | ((gm == gate) & (mth < n_iota))
                beaten = beaten + wins.astype(jnp.int32)
            bias = jnp.where(beaten < MOBA_TOPK, 0.0, MASKED)
            for j in range(qi):
                blocks[j] = blocks[j] + bias[j:j + 1, :]

        ps, l = _prefix_softmax(blocks)
        p = jnp.concatenate([pj.astype(BF16) for pj in ps], axis=0)
        acc = jnp.dot(vt_ref[:, 0:nk], p, preferred_element_type=F32)
        o_ref[0, rows, :] = (acc / l).T.astype(o_ref.dtype)


def _moba_attention(proj, gq, gk, tables):
    b, s, _ = proj.shape
    nb = s // MOBA_BLOCK
    c, sa, sb = tables
    head = lambda off: pl.BlockSpec((1, s, HEAD_DIM), lambda bi, h: (bi, 0, off + h))
    full = lambda shape: pl.BlockSpec(shape, lambda bi, h: (0,) * len(shape))
    return pl.pallas_call(
        _moba_kernel,
        grid=(b, MOBA_HEADS),
        in_specs=[head(0), head(MOBA_HEADS), head(2 * MOBA_HEADS),
                  full((1, HEAD_DIM)), full((1, HEAD_DIM)),
                  full((s, HEAD_DIM)), full((s, HEAD_DIM)), full((s, HEAD_DIM))],
        out_specs=pl.BlockSpec((1, s, HEAD_DIM), lambda bi, h: (bi, 0, h)),
        out_shape=jax.ShapeDtypeStruct((b, s, MOBA_WIDTH), BF16),
        scratch_shapes=[
            pltpu.VMEM((s, HEAD_DIM), BF16),
            pltpu.VMEM((HEAD_DIM, s), BF16),
            pltpu.VMEM((nb, HEAD_DIM), F32),
        ],
        compiler_params=_params(2),
        name="moba_attention",
    )(proj, proj, proj, gq.reshape(1, -1).astype(F32), gk.reshape(1, -1).astype(F32), c, sa, sb)


def _diff_kernel(lam_init, q_ref, k_ref, v_ref, gq_ref, gk_ref, lq1_ref, lk1_ref, lq2_ref, lk2_ref,
                 gs_ref, c_ref, sa_ref, sb_ref, o_ref, k1_ref, k2_ref, vt_ref):
    blk = MOBA_BLOCK
    nb = q_ref.shape[1] // blk
    hd = HEAD_DIM

    lam = (jnp.exp(jnp.sum(lq1_ref[...] * lk1_ref[...], axis=-1, keepdims=True))
           - jnp.exp(jnp.sum(lq2_ref[...] * lk2_ref[...], axis=-1, keepdims=True))
           + lam_init)

    def halves(ref, rows, g):
        tabs = (c_ref[rows, :], sa_ref[rows, :], sb_ref[rows, :])
        x1 = _norm_rope(ref[0, rows, 0:hd].astype(F32), g, *tabs)
        x2 = _norm_rope(ref[0, rows, hd:2 * hd].astype(F32), g, *tabs)
        return x1.astype(BF16), x2.astype(BF16)

    for n in range(nb):
        rows = slice(n * blk, (n + 1) * blk)
        k1_ref[rows, :], k2_ref[rows, :] = halves(k_ref, rows, gk_ref[...])
        vt_ref[:, rows] = v_ref[0, rows, :].astype(F32).T.astype(BF16)

    for qi in range(nb):
        rows = slice(qi * blk, (qi + 1) * blk)
        nk = (qi + 1) * blk
        q1, q2 = halves(q_ref, rows, gq_ref[...])

        maps = []
        for kref, qh in ((k1_ref, q1), (k2_ref, q2)):
            s = lax.dot_general(kref[0:nk, :], qh, NT_DIMS, preferred_element_type=F32)
            blocks = _split_blocks(s, qi + 1)
            blocks[qi] = _causal_mask(blocks[qi])
            maps.append(_prefix_softmax(blocks))
        (p1, l1), (p2, l2) = maps
        r1 = 1.0 / l1
        r2 = lam / l2
        w = jnp.concatenate([(a * r1 - b * r2).astype(BF16) for a, b in zip(p1, p2)], axis=0)
        o = jnp.dot(vt_ref[:, 0:nk], w, preferred_element_type=F32)
        ms = jnp.mean(o * o, axis=0, keepdims=True)
        on = (o * lax.rsqrt(ms + NORM_EPS)).T * gs_ref[...]
        o_ref[0, rows, :] = (on * (1.0 - lam_init)).astype(o_ref.dtype)


def _diff_attention(proj, gq, gk, lq1, lk1, lq2, lk2, gs, lam_init, tables):
    b, s, _ = proj.shape
    c, sa, sb = tables
    qoff = 3 * MOBA_WIDTH // DIFF_V_DIM
    head = lambda off: pl.BlockSpec((1, s, DIFF_V_DIM), lambda bi, h: (bi, 0, off + h))
    full = lambda shape: pl.BlockSpec(shape, lambda bi, h: (0,) * len(shape))
    row = lambda v: v.reshape(1, -1).astype(F32)
    return pl.pallas_call(
        functools.partial(_diff_kernel, lam_init),
        grid=(b, DIFF_HEADS),
        in_specs=[head(qoff), head(qoff + DIFF_HEADS), head(qoff + 2 * DIFF_HEADS),
                  full((1, HEAD_DIM)), full((1, HEAD_DIM)),
                  full((1, HEAD_DIM)), full((1, HEAD_DIM)), full((1, HEAD_DIM)), full((1, HEAD_DIM)),
                  full((1, DIFF_V_DIM)),
                  full((s, HEAD_DIM)), full((s, HEAD_DIM)), full((s, HEAD_DIM))],
        out_specs=pl.BlockSpec((1, s, DIFF_V_DIM), lambda bi, h: (bi, 0, h)),
        out_shape=jax.ShapeDtypeStruct((b, s, DIFF_HEADS * DIFF_V_DIM), BF16),
        scratch_shapes=[
            pltpu.VMEM((s, HEAD_DIM), BF16),
            pltpu.VMEM((s, HEAD_DIM), BF16),
            pltpu.VMEM((DIFF_V_DIM, s), BF16),
        ],
        compiler_params=_params(2),
        name="diff_attention",
    )(proj, proj, proj, row(gq), row(gk), row(lq1), row(lk1), row(lq2), row(lk2), row(gs), c, sa, sb)


def _mem_kv_kernel(m_ref, wk_ref, wv_ref, gk_ref, k_ref, v_ref):
    x = m_ref[...]
    k = jnp.dot(x, wk_ref[...], preferred_element_type=F32)
    for h in range(MEM_HEADS):
        cols = slice(h * HEAD_DIM, (h + 1) * HEAD_DIM)
        kh = k[:, cols]
        ms = jnp.mean(kh * kh, axis=-1, keepdims=True)
        k_ref[:, cols] = (kh * lax.rsqrt(ms + NORM_EPS) * gk_ref[...]).astype(k_ref.dtype)
    v_ref[...] = jnp.dot(x, wv_ref[...], preferred_element_type=F32).astype(v_ref.dtype)


def _mem_kv(mem_n, wk, wv, gk, tm):
    m, d = mem_n.shape
    return pl.pallas_call(
        _mem_kv_kernel,
        grid=(m // tm,),
        in_specs=[pl.BlockSpec((tm, d), lambda i: (i, 0)),
                  pl.BlockSpec((d, MEM_WIDTH), lambda i: (0, 0)),
                  pl.BlockSpec((d, MEM_WIDTH), lambda i: (0, 0)),
                  pl.BlockSpec((1, HEAD_DIM), lambda i: (0, 0))],
        out_specs=[pl.BlockSpec((tm, MEM_WIDTH), lambda i: (i, 0)),
                   pl.BlockSpec((tm, MEM_WIDTH), lambda i: (i, 0))],
        out_shape=[jax.ShapeDtypeStruct((m, MEM_WIDTH), BF16), jax.ShapeDtypeStruct((m, MEM_WIDTH), BF16)],
        compiler_params=_params(1),
        name="mem_kv",
    )(mem_n, wk, wv, gk.reshape(1, -1).astype(F32))


def _cross_kernel(hn_ref, h_ref, k_ref, v_ref, wq_ref, wo_ref, gq_ref, o_ref):
    q = jnp.dot(hn_ref[...], wq_ref[...], preferred_element_type=F32)
    heads = []
    for h in range(MEM_HEADS):
        cols = slice(h * HEAD_DIM, (h + 1) * HEAD_DIM)
        qh = q[:, cols]
        ms = jnp.mean(qh * qh, axis=-1, keepdims=True)
        qh = (qh * lax.rsqrt(ms + NORM_EPS) * gq_ref[...]).astype(BF16)
        s = lax.dot_general(qh, k_ref[0, :, cols], NT_DIMS, preferred_element_type=F32) * ATTN_SCALE
        e = jnp.exp(s - jnp.max(s, axis=-1, keepdims=True))
        p = (e / jnp.sum(e, axis=-1, keepdims=True)).astype(BF16)
        heads.append(jnp.dot(p, v_ref[0, :, cols], preferred_element_type=F32).astype(BF16))
    o = jnp.concatenate(heads, axis=-1)
    o_ref[...] = h_ref[...] + jnp.dot(o, wo_ref[...], preferred_element_type=F32)


def _cross_attention(hn, h, k, v, wq, wo, gq, seq, tm):
    m, d = hn.shape
    mem_len = k.shape[1]
    per_batch = seq // tm
    return pl.pallas_call(
        _cross_kernel,
        grid=(m // tm,),
        in_specs=[pl.BlockSpec((tm, d), lambda i: (i, 0)),
                  pl.BlockSpec((tm, d), lambda i: (i, 0)),
                  pl.BlockSpec((1, mem_len, MEM_WIDTH), lambda i: (i // per_batch, 0, 0)),
                  pl.BlockSpec((1, mem_len, MEM_WIDTH), lambda i: (i // per_batch, 0, 0)),
                  pl.BlockSpec((d, MEM_WIDTH), lambda i: (0, 0)),
                  pl.BlockSpec((MEM_WIDTH, d), lambda i: (0, 0)),
                  pl.BlockSpec((1, HEAD_DIM), lambda i: (0, 0))],
        out_specs=pl.BlockSpec((tm, d), lambda i: (i, 0)),
        out_shape=jax.ShapeDtypeStruct((m, d), F32),
        compiler_params=_params(1),
        name="cross_attention",
    )(hn, h, k, v, wq, wo, gq.reshape(1, -1).astype(F32))


def _layer(h, mem_n_src, l, p):
    b, s, d = h.shape
    m = b * s
    lam_init = 0.8 - 0.6 * math.exp(-0.3 * l)
    tables = _rope_tables(s)
    bf = lambda w: w.astype(BF16)
    h2d = h.reshape(m, d)

    xn = _rmsnorm(h2d, p["norm_mix_g"][l], 256)
    proj = _matmul(xn, bf(p["w_in"][l]), 1024, 1024, BF16).reshape(b, s, IN_WIDTH)
    out_a = _moba_attention(proj, p["q_norm_a"][l], p["k_norm_a"][l], tables)
    out_b = _diff_attention(proj, p["q_norm_b"][l], p["k_norm_b"][l], p["lam_q1"][l], p["lam_k1"][l],
                            p["lam_q2"][l], p["lam_k2"][l], p["diff_subln_g"][l], lam_init, tables)
    h2d = _out_projection(out_a.reshape(m, -1), out_b.reshape(m, -1), bf(p["w_out"][l]), h2d, 1024, 512)

    hn = _rmsnorm(h2d, p["norm_cross_g"][l], 256)
    mem_len = mem_n_src.shape[1]
    mem_n = _rmsnorm(mem_n_src.reshape(b * mem_len, d), p["norm_mem_g"][l], 256)
    mk, mv = _mem_kv(mem_n, bf(p["w_mk"][l]), bf(p["w_mv"][l]), p["k_norm_m"][l], 256)
    h2d = _cross_attention(hn, h2d, mk.reshape(b, mem_len, -1), mv.reshape(b, mem_len, -1),
                           bf(p["w_mq"][l]), bf(p["w_mo"][l]), p["q_norm_m"][l], s, 256)

    hn = _rmsnorm(h2d, p["norm_ffn_g"][l], 256)
    act = _swiglu_up(hn, bf(p["w_gate"][l]), bf(p["w_up"][l]), 1024, 256)
    h2d = _matmul_residual(act, bf(p["w_down"][l]), h2d, 512, 512)
    return h2d.reshape(b, s, d)


def kernel(x, mem, norm_mix_g, w_in, q_norm_a, k_norm_a, q_norm_b, k_norm_b, lam_q1, lam_k1, lam_q2, lam_k2, diff_subln_g, w_out, norm_cross_g, norm_mem_g, w_mq, w_mk, w_mv, w_mo, q_norm_m, k_norm_m, norm_ffn_g, w_gate, w_up, w_down):
    p = dict(norm_mix_g=norm_mix_g, w_in=w_in, q_norm_a=q_norm_a, k_norm_a=k_norm_a, q_norm_b=q_norm_b,
             k_norm_b=k_norm_b, lam_q1=lam_q1, lam_k1=lam_k1, lam_q2=lam_q2, lam_k2=lam_k2,
             diff_subln_g=diff_subln_g, w_out=w_out, norm_cross_g=norm_cross_g, norm_mem_g=norm_mem_g,
             w_mq=w_mq, w_mk=w_mk, w_mv=w_mv, w_mo=w_mo, q_norm_m=q_norm_m, k_norm_m=k_norm_m,
             norm_ffn_g=norm_ffn_g, w_gate=w_gate, w_up=w_up, w_down=w_down)
    h = x
    for l in range(w_in.shape[0]):
        h = _layer(h, mem, l, p)
    return h
```

```python
import functools
import math

import jax
import jax.numpy as jnp
from jax import lax
from jax.experimental import pallas as pl
from jax.experimental.pallas import tpu as pltpu

F32 = jnp.float32
BF16 = jnp.bfloat16

D_MODEL = 4096
HEAD_DIM = 128
MOBA_HEADS = 16
DIFF_HEADS = 8
MOBA_WIDTH = MOBA_HEADS * HEAD_DIM
DIFF_V_DIM = 2 * HEAD_DIM
IN_WIDTH = 12288
MOBA_BLOCK = 256
MOBA_TOPK = 3
ROPE_THETA = 500000.0
ROT_DIM = HEAD_DIM // 4
MEM_HEADS = 4
MEM_WIDTH = MEM_HEADS * HEAD_DIM
NORM_EPS = 1e-6
ATTN_SCALE = HEAD_DIM ** -0.5
MASKED = -1e30
EXP2_SCALE = ATTN_SCALE * math.log2(math.e)

V7X_VMEM_BYTES = 64 * 1024 * 1024
VMEM_LIMIT = V7X_VMEM_BYTES - 8 * 1024 * 1024

NT_DIMS = (((1,), (1,)), ((), ()))


def _params(n_grid):
    return pltpu.CompilerParams(
        dimension_semantics=("parallel",) * (n_grid - 1) + ("arbitrary",),
        vmem_limit_bytes=VMEM_LIMIT,
    )


def _rmsnorm_kernel(x_ref, g_ref, o_ref):
    x = x_ref[...].astype(F32)
    ms = jnp.mean(x * x, axis=-1, keepdims=True)
    o_ref[...] = (x * lax.rsqrt(ms + NORM_EPS) * g_ref[...]).astype(o_ref.dtype)


def _rmsnorm(x, g, tm):
    m, d = x.shape
    return pl.pallas_call(
        _rmsnorm_kernel,
        grid=(m // tm,),
        in_specs=[pl.BlockSpec((tm, d), lambda i: (i, 0)), pl.BlockSpec((1, d), lambda i: (0, 0))],
        out_specs=pl.BlockSpec((tm, d), lambda i: (i, 0)),
        out_shape=jax.ShapeDtypeStruct((m, d), BF16),
        compiler_params=_params(1),
        name="rmsnorm",
    )(x, g.reshape(1, d).astype(F32))


def _mm_kernel(x_ref, w_ref, o_ref):
    o_ref[...] = jnp.dot(x_ref[...], w_ref[...], preferred_element_type=F32).astype(o_ref.dtype)


def _matmul(x, w, tm, tn, out_dtype):
    m, k = x.shape
    n = w.shape[1]
    return pl.pallas_call(
        _mm_kernel,
        grid=(m // tm, n // tn),
        in_specs=[pl.BlockSpec((tm, k), lambda i, j: (i, 0)), pl.BlockSpec((k, tn), lambda i, j: (0, j))],
        out_specs=pl.BlockSpec((tm, tn), lambda i, j: (i, j)),
        out_shape=jax.ShapeDtypeStruct((m, n), out_dtype),
        compiler_params=_params(2),
        name="matmul",
    )(x, w)


def _mm_res_kernel(x_ref, w_ref, r_ref, o_ref):
    o_ref[...] = r_ref[...] + jnp.dot(x_ref[...], w_ref[...], preferred_element_type=F32)


def _matmul_residual(x, w, r, tm, tn):
    m, k = x.shape
    n = w.shape[1]
    return pl.pallas_call(
        _mm_res_kernel,
        grid=(m // tm, n // tn),
        in_specs=[
            pl.BlockSpec((tm, k), lambda i, j: (i, 0)),
            pl.BlockSpec((k, tn), lambda i, j: (0, j)),
            pl.BlockSpec((tm, tn), lambda i, j: (i, j)),
        ],
        out_specs=pl.BlockSpec((tm, tn), lambda i, j: (i, j)),
        out_shape=jax.ShapeDtypeStruct((m, n), F32),
        compiler_params=_params(2),
        name="matmul_residual",
    )(x, w, r)


def _mm2_res_kernel(a_ref, b_ref, wa_ref, wb_ref, r_ref, o_ref):
    acc = jnp.dot(a_ref[...], wa_ref[...], preferred_element_type=F32)
    acc = acc + jnp.dot(b_ref[...], wb_ref[...], preferred_element_type=F32)
    o_ref[...] = r_ref[...] + acc


def _out_projection(a, b, w, r, tm, tn):
    m, ka = a.shape
    kb = b.shape[1]
    assert ka == kb and w.shape[0] == ka + kb
    n = w.shape[1]
    return pl.pallas_call(
        _mm2_res_kernel,
        grid=(m // tm, n // tn),
        in_specs=[
            pl.BlockSpec((tm, ka), lambda i, j: (i, 0)),
            pl.BlockSpec((tm, kb), lambda i, j: (i, 0)),
            pl.BlockSpec((ka, tn), lambda i, j: (0, j)),
            pl.BlockSpec((kb, tn), lambda i, j: (1, j)),
            pl.BlockSpec((tm, tn), lambda i, j: (i, j)),
        ],
        out_specs=pl.BlockSpec((tm, tn), lambda i, j: (i, j)),
        out_shape=jax.ShapeDtypeStruct((m, n), F32),
        compiler_params=_params(2),
        name="out_projection",
    )(a, b, w, w, r)


def _swiglu_kernel(x_ref, wg_ref, wu_ref, o_ref):
    x = x_ref[...]
    g = jnp.dot(x, wg_ref[...], preferred_element_type=F32)
    u = jnp.dot(x, wu_ref[...], preferred_element_type=F32)
    o_ref[...] = (jax.nn.silu(g) * u).astype(o_ref.dtype)


def _swiglu_up(x, wg, wu, tm, tn):
    m, k = x.shape
    n = wg.shape[1]
    return pl.pallas_call(
        _swiglu_kernel,
        grid=(m // tm, n // tn),
        in_specs=[
            pl.BlockSpec((tm, k), lambda i, j: (i, 0)),
            pl.BlockSpec((k, tn), lambda i, j: (0, j)),
            pl.BlockSpec((k, tn), lambda i, j: (0, j)),
        ],
        out_specs=pl.BlockSpec((tm, tn), lambda i, j: (i, j)),
        out_shape=jax.ShapeDtypeStruct((m, n), BF16),
        compiler_params=_params(2),
        name="swiglu_up",
    )(x, wg, wu)


def _rope_tables(seq):
    half = ROT_DIM // 2
    pos = jnp.arange(seq, dtype=F32)
    inv = ROPE_THETA ** (-jnp.arange(0, ROT_DIM, 2, dtype=F32) / ROT_DIM)
    ang = pos[:, None] * inv[None, :]
    cos, sin = jnp.cos(ang), jnp.sin(ang)
    rest = HEAD_DIM - ROT_DIM
    c = jnp.concatenate([cos, cos, jnp.ones((seq, rest), F32)], axis=-1)
    sa = jnp.concatenate([-sin, jnp.zeros((seq, HEAD_DIM - half), F32)], axis=-1)
    sb = jnp.concatenate([jnp.zeros((seq, half), F32), sin, jnp.zeros((seq, rest), F32)], axis=-1)
    return c, sa, sb


def _norm_rope(x, g, c, sa, sb):
    half = ROT_DIM // 2
    ms = jnp.mean(x * x, axis=-1, keepdims=True)
    y = x * lax.rsqrt(ms + NORM_EPS) * g
    up = pltpu.roll(y, HEAD_DIM - half, 1)
    down = pltpu.roll(y, half, 1)
    return y * c + up * sa + down * sb


def _causal_mask(s):
    key = lax.broadcasted_iota(jnp.int32, s.shape, 0)
    qry = lax.broadcasted_iota(jnp.int32, s.shape, 1)
    return jnp.where(key <= qry, s, MASKED)


def _prefix_softmax(blocks):
    m = jnp.max(blocks[0], axis=0, keepdims=True)
    for sj in blocks[1:]:
        m = jnp.maximum(m, jnp.max(sj, axis=0, keepdims=True))
    ps, l = [], None
    for sj in blocks:
        pj = jnp.exp2((sj - m) * EXP2_SCALE)
        lj = jnp.sum(pj, axis=0, keepdims=True)
        l = lj if l is None else l + lj
        ps.append(pj)
    return ps, l


def _split_blocks(s, n):
    return [s[j * MOBA_BLOCK:(j + 1) * MOBA_BLOCK, :] for j in range(n)]


def _moba_kernel(q_ref, k_ref, v_ref, gq_ref, gk_ref, c_ref, sa_ref, sb_ref, o_ref,
                 kn_ref, vt_ref, kmean_ref):
    blk = MOBA_BLOCK
    nb = q_ref.shape[1] // blk

    for n in range(nb):
        rows = slice(n * blk, (n + 1) * blk)
        kn = _norm_rope(k_ref[0, rows, :].astype(F32), gk_ref[...],
                        c_ref[rows, :], sa_ref[rows, :], sb_ref[rows, :])
        kn_ref[rows, :] = kn.astype(BF16)
        kmean_ref[n:n + 1, :] = jnp.mean(kn, axis=0, keepdims=True)
        vt_ref[:, rows] = v_ref[0, rows, :].astype(F32).T.astype(BF16)

    for qi in range(nb):
        rows = slice(qi * blk, (qi + 1) * blk)
        nk = (qi + 1) * blk
        qn = _norm_rope(q_ref[0, rows, :].astype(F32), gq_ref[...],
                        c_ref[rows, :], sa_ref[rows, :], sb_ref[rows, :])
        s = lax.dot_general(kn_ref[0:nk, :], qn.astype(BF16), NT_DIMS, preferred_element_type=F32)
        blocks = _split_blocks(s, qi + 1)
        blocks[qi] = _causal_mask(blocks[qi])

        if qi > MOBA_TOPK:
            gate = lax.dot_general(kmean_ref[...], qn, NT_DIMS, preferred_element_type=F32,
                                   precision=lax.Precision.HIGHEST)
            n_iota = lax.broadcasted_iota(jnp.int32, gate.shape, 0)
            beaten = jnp.zeros(gate.shape, jnp.int32)
            for mth in range(qi):
                gm = gate[mth:mth + 1, :]
                wins = (gm > gate) | ((gm == gate) & (mth < n_iota))
                beaten = beaten + wins.astype(jnp.int32)
            bias = jnp.where(beaten < MOBA_TOPK, 0.0, MASKED)
            for j in range(qi):
                blocks[j] = blocks[j] + bias[j:j + 1, :]

        ps, l = _prefix_softmax(blocks)
        p = jnp.concatenate([pj.astype(BF16) for pj in ps], axis=0)
        acc = jnp.dot(vt_ref[:, 0:nk], p, preferred_element_type=F32)
        o_ref[0, rows, :] = (acc / l).T.astype(o_ref.dtype)


def _moba_attention(proj, gq, gk, tables):
    b, s, _ = proj.shape
    nb = s // MOBA_BLOCK
    c, sa, sb = tables
    head = lambda off: pl.BlockSpec((1, s, HEAD_DIM), lambda bi, h: (bi, 0, off + h))
    full = lambda shape: pl.BlockSpec(shape, lambda bi, h: (0,) * len(shape))
    return pl.pallas_call(
        _moba_kernel,
        grid=(b, MOBA_HEADS),
        in_specs=[head(0), head(MOBA_HEADS), head(2 * MOBA_HEADS),
                  full((1, HEAD_DIM)), full((1, HEAD_DIM)),
                  full((s, HEAD_DIM)), full((s, HEAD_DIM)), full((s, HEAD_DIM))],
        out_specs=pl.BlockSpec((1, s, HEAD_DIM), lambda bi, h: (bi, 0, h)),
        out_shape=jax.ShapeDtypeStruct((b, s, MOBA_WIDTH), BF16),
        scratch_shapes=[
            pltpu.VMEM((s, HEAD_DIM), BF16),
            pltpu.VMEM((HEAD_DIM, s), BF16),
            pltpu.VMEM((nb, HEAD_DIM), F32),
        ],
        compiler_params=_params(2),
        name="moba_attention",
    )(proj, proj, proj, gq.reshape(1, -1).astype(F32), gk.reshape(1, -1).astype(F32), c, sa, sb)


def _diff_kernel(lam_init, q_ref, k_ref, v_ref, gq_ref, gk_ref, lq1_ref, lk1_ref, lq2_ref, lk2_ref,
                 gs_ref, c_ref, sa_ref, sb_ref, o_ref, k1_ref, k2_ref, vt_ref):
    blk = MOBA_BLOCK
    nb = q_ref.shape[1] // blk
    hd = HEAD_DIM

    lam = (jnp.exp(jnp.sum(lq1_ref[...] * lk1_ref[...], axis=-1, keepdims=True))
           - jnp.exp(jnp.sum(lq2_ref[...] * lk2_ref[...], axis=-1, keepdims=True))
           + lam_init)

    def halves(ref, rows, g):
        tabs = (c_ref[rows, :], sa_ref[rows, :], sb_ref[rows, :])
        x1 = _norm_rope(ref[0, rows, 0:hd].astype(F32), g, *tabs)
        x2 = _norm_rope(ref[0, rows, hd:2 * hd].astype(F32), g, *tabs)
        return x1.astype(BF16), x2.astype(BF16)

    for n in range(nb):
        rows = slice(n * blk, (n + 1) * blk)
        k1_ref[rows, :], k2_ref[rows, :] = halves(k_ref, rows, gk_ref[...])
        vt_ref[:, rows] = v_ref[0, rows, :].astype(F32).T.astype(BF16)

    for qi in range(nb):
        rows = slice(qi * blk, (qi + 1) * blk)
        nk = (qi + 1) * blk
        q1, q2 = halves(q_ref, rows, gq_ref[...])

        maps = []
        for kref, qh in ((k1_ref, q1), (k2_ref, q2)):
            s = lax.dot_general(kref[0:nk, :], qh, NT_DIMS, preferred_element_type=F32)
            blocks = _split_blocks(s, qi + 1)
            blocks[qi] = _causal_mask(blocks[qi])
            maps.append(_prefix_softmax(blocks))
        (p1, l1), (p2, l2) = maps
        r1 = 1.0 / l1
        r2 = lam / l2
        w = jnp.concatenate([(a * r1 - b * r2).astype(BF16) for a, b in zip(p1, p2)], axis=0)
        o = jnp.dot(vt_ref[:, 0:nk], w, preferred_element_type=F32)
        ms = jnp.mean(o * o, axis=0, keepdims=True)
        on = (o * lax.rsqrt(ms + NORM_EPS)).T * gs_ref[...]
        o_ref[0, rows, :] = (on * (1.0 - lam_init)).astype(o_ref.dtype)


def _diff_attention(proj, gq, gk, lq1, lk1, lq2, lk2, gs, lam_init, tables):
    b, s, _ = proj.shape
    c, sa, sb = tables
    qoff = 3 * MOBA_WIDTH // DIFF_V_DIM
    head = lambda off: pl.BlockSpec((1, s, DIFF_V_DIM), lambda bi, h: (bi, 0, off + h))
    full = lambda shape: pl.BlockSpec(shape, lambda bi, h: (0,) * len(shape))
    row = lambda v: v.reshape(1, -1).astype(F32)
    return pl.pallas_call(
        functools.partial(_diff_kernel, lam_init),
        grid=(b, DIFF_HEADS),
        in_specs=[head(qoff), head(qoff + DIFF_HEADS), head(qoff + 2 * DIFF_HEADS),
                  full((1, HEAD_DIM)), full((1, HEAD_DIM)),
                  full((1, HEAD_DIM)), full((1, HEAD_DIM)), full((1, HEAD_DIM)), full((1, HEAD_DIM)),
                  full((1, DIFF_V_DIM)),
                  full((s, HEAD_DIM)), full((s, HEAD_DIM)), full((s, HEAD_DIM))],
        out_specs=pl.BlockSpec((1, s, DIFF_V_DIM), lambda bi, h: (bi, 0, h)),
        out_shape=jax.ShapeDtypeStruct((b, s, DIFF_HEADS * DIFF_V_DIM), BF16),
        scratch_shapes=[
            pltpu.VMEM((s, HEAD_DIM), BF16),
            pltpu.VMEM((s, HEAD_DIM), BF16),
            pltpu.VMEM((DIFF_V_DIM, s), BF16),
        ],
        compiler_params=_params(2),
        name="diff_attention",
    )(proj, proj, proj, row(gq), row(gk), row(lq1), row(lk1), row(lq2), row(lk2), row(gs), c, sa, sb)


def _mem_kv_kernel(m_ref, wk_ref, wv_ref, gk_ref, k_ref, v_ref):
    x = m_ref[...]
    k = jnp.dot(x, wk_ref[...], preferred_element_type=F32)
    for h in range(MEM_HEADS):
        cols = slice(h * HEAD_DIM, (h + 1) * HEAD_DIM)
        kh = k[:, cols]
        ms = jnp.mean(kh * kh, axis=-1, keepdims=True)
        k_ref[:, cols] = (kh * lax.rsqrt(ms + NORM_EPS) * gk_ref[...]).astype(k_ref.dtype)
    v_ref[...] = jnp.dot(x, wv_ref[...], preferred_element_type=F32).astype(v_ref.dtype)


def _mem_kv(mem_n, wk, wv, gk, tm):
    m, d = mem_n.shape
    return pl.pallas_call(
        _mem_kv_kernel,
        grid=(m // tm,),
        in_specs=[pl.BlockSpec((tm, d), lambda i: (i, 0)),
                  pl.BlockSpec((d, MEM_WIDTH), lambda i: (0, 0)),
                  pl.BlockSpec((d, MEM_WIDTH), lambda i: (0, 0)),
                  pl.BlockSpec((1, HEAD_DIM), lambda i: (0, 0))],
        out_specs=[pl.BlockSpec((tm, MEM_WIDTH), lambda i: (i, 0)),
                   pl.BlockSpec((tm, MEM_WIDTH), lambda i: (i, 0))],
        out_shape=[jax.ShapeDtypeStruct((m, MEM_WIDTH), BF16), jax.ShapeDtypeStruct((m, MEM_WIDTH), BF16)],
        compiler_params=_params(1),
        name="mem_kv",
    )(mem_n, wk, wv, gk.reshape(1, -1).astype(F32))


def _row_rmsnorm(x, g):
    ms = jnp.mean(x * x, axis=-1, keepdims=True)
    return x * lax.rsqrt(ms + NORM_EPS) * g


def _cross_kernel(h_ref, k_ref, v_ref, wq_ref, wo_ref, gq_ref, gin_ref, gout_ref, o_ref, on_ref):
    hn = _row_rmsnorm(h_ref[...], gin_ref[...]).astype(BF16)
    q = jnp.dot(hn, wq_ref[...], preferred_element_type=F32)
    heads = []
    for h in range(MEM_HEADS):
        cols = slice(h * HEAD_DIM, (h + 1) * HEAD_DIM)
        qh = q[:, cols]
        ms = jnp.mean(qh * qh, axis=-1, keepdims=True)
        qh = (qh * lax.rsqrt(ms + NORM_EPS) * gq_ref[...]).astype(BF16)
        s = lax.dot_general(qh, k_ref[0, :, cols], NT_DIMS, preferred_element_type=F32) * ATTN_SCALE
        e = jnp.exp(s - jnp.max(s, axis=-1, keepdims=True))
        p = (e / jnp.sum(e, axis=-1, keepdims=True)).astype(BF16)
        heads.append(jnp.dot(p, v_ref[0, :, cols], preferred_element_type=F32).astype(BF16))
    o = jnp.concatenate(heads, axis=-1)
    out = h_ref[...] + jnp.dot(o, wo_ref[...], preferred_element_type=F32)
    o_ref[...] = out
    on_ref[...] = _row_rmsnorm(out, gout_ref[...]).astype(on_ref.dtype)


def _cross_attention(h, k, v, wq, wo, gq, g_in, g_out, seq, tm):
    m, d = h.shape
    mem_len = k.shape[1]
    per_batch = seq // tm
    row = lambda g: g.reshape(1, -1).astype(F32)
    return pl.pallas_call(
        _cross_kernel,
        grid=(m // tm,),
        in_specs=[pl.BlockSpec((tm, d), lambda i: (i, 0)),
                  pl.BlockSpec((1, mem_len, MEM_WIDTH), lambda i: (i // per_batch, 0, 0)),
                  pl.BlockSpec((1, mem_len, MEM_WIDTH), lambda i: (i // per_batch, 0, 0)),
                  pl.BlockSpec((d, MEM_WIDTH), lambda i: (0, 0)),
                  pl.BlockSpec((MEM_WIDTH, d), lambda i: (0, 0)),
                  pl.BlockSpec((1, HEAD_DIM), lambda i: (0, 0)),
                  pl.BlockSpec((1, d), lambda i: (0, 0)),
                  pl.BlockSpec((1, d), lambda i: (0, 0))],
        out_specs=[pl.BlockSpec((tm, d), lambda i: (i, 0)), pl.BlockSpec((tm, d), lambda i: (i, 0))],
        out_shape=[jax.ShapeDtypeStruct((m, d), F32), jax.ShapeDtypeStruct((m, d), BF16)],
        compiler_params=_params(1),
        name="cross_attention",
    )(h, k, v, wq, wo, row(gq), row(g_in), row(g_out))


def _layer(h, mem_n_src, l, p):
    b, s, d = h.shape
    m = b * s
    lam_init = 0.8 - 0.6 * math.exp(-0.3 * l)
    tables = _rope_tables(s)
    bf = lambda w: w.astype(BF16)
    h2d = h.reshape(m, d)

    xn = _rmsnorm(h2d, p["norm_mix_g"][l], 256)
    proj = _matmul(xn, bf(p["w_in"][l]), 1024, 1024, BF16).reshape(b, s, IN_WIDTH)
    out_a = _moba_attention(proj, p["q_norm_a"][l], p["k_norm_a"][l], tables)
    out_b = _diff_attention(proj, p["q_norm_b"][l], p["k_norm_b"][l], p["lam_q1"][l], p["lam_k1"][l],
                            p["lam_q2"][l], p["lam_k2"][l], p["diff_subln_g"][l], lam_init, tables)
    h2d = _out_projection(out_a.reshape(m, -1), out_b.reshape(m, -1), bf(p["w_out"][l]), h2d, 1024, 512)

    mem_len = mem_n_src.shape[1]
    mem_n = _rmsnorm(mem_n_src.reshape(b * mem_len, d), p["norm_mem_g"][l], 256)
    mk, mv = _mem_kv(mem_n, bf(p["w_mk"][l]), bf(p["w_mv"][l]), p["k_norm_m"][l], 256)
    h2d, hn = _cross_attention(h2d, mk.reshape(b, mem_len, -1), mv.reshape(b, mem_len, -1),
                               bf(p["w_mq"][l]), bf(p["w_mo"][l]), p["q_norm_m"][l],
                               p["norm_cross_g"][l], p["norm_ffn_g"][l], s, 256)

    act = _swiglu_up(hn, bf(p["w_gate"][l]), bf(p["w_up"][l]), 1024, 256)
    h2d = _matmul_residual(act, bf(p["w_down"][l]), h2d, 512, 512)
    return h2d.reshape(b, s, d)


def kernel(x, mem, norm_mix_g, w_in, q_norm_a, k_norm_a, q_norm_b, k_norm_b, lam_q1, lam_k1, lam_q2, lam_k2, diff_subln_g, w_out, norm_cross_g, norm_mem_g, w_mq, w_mk, w_mv, w_mo, q_norm_m, k_norm_m, norm_ffn_g, w_gate, w_up, w_down):
    p = dict(norm_mix_g=norm_mix_g, w_in=w_in, q_norm_a=q_norm_a, k_norm_a=k_norm_a, q_norm_b=q_norm_b,
             k_norm_b=k_norm_b, lam_q1=lam_q1, lam_k1=lam_k1, lam_q2=lam_q2, lam_k2=lam_k2,
             diff_subln_g=diff_subln_g, w_out=w_out, norm_cross_g=norm_cross_g, norm_mem_g=norm_mem_g,
             w_mq=w_mq, w_mk=w_mk, w_mv=w_mv, w_mo=w_mo, q_norm_m=q_norm_m, k_norm_m=k_norm_m,
             norm_ffn_g=norm_ffn_g, w_gate=w_gate, w_up=w_up, w_down=w_down)
    h = x
    for l in range(w_in.shape[0]):
        h = _layer(h, mem, l, p)
    return h
```

```python
import functools
import math

import jax
import jax.numpy as jnp
from jax import lax
from jax.experimental import pallas as pl
from jax.experimental.pallas import tpu as pltpu

F32 = jnp.float32
BF16 = jnp.bfloat16

D_MODEL = 4096
HEAD_DIM = 128
MOBA_HEADS = 16
DIFF_HEADS = 8
MOBA_WIDTH = MOBA_HEADS * HEAD_DIM
DIFF_V_DIM = 2 * HEAD_DIM
IN_WIDTH = 12288
MOBA_BLOCK = 256
MOBA_TOPK = 3
ROPE_THETA = 500000.0
ROT_DIM = HEAD_DIM // 4
MEM_HEADS = 4
MEM_WIDTH = MEM_HEADS * HEAD_DIM
NORM_EPS = 1e-6
ATTN_SCALE = HEAD_DIM ** -0.5
MASKED = -1e30
EXP2_SCALE = ATTN_SCALE * math.log2(math.e)

V7X_VMEM_BYTES = 64 * 1024 * 1024
VMEM_LIMIT = V7X_VMEM_BYTES - 8 * 1024 * 1024

NT_DIMS = (((1,), (1,)), ((), ()))


def _params(n_grid):
    return pltpu.CompilerParams(
        dimension_semantics=("parallel",) * (n_grid - 1) + ("arbitrary",),
        vmem_limit_bytes=VMEM_LIMIT,
    )


def _rmsnorm_kernel(x_ref, g_ref, o_ref):
    x = x_ref[...].astype(F32)
    ms = jnp.mean(x * x, axis=-1, keepdims=True)
    o_ref[...] = (x * lax.rsqrt(ms + NORM_EPS) * g_ref[...]).astype(o_ref.dtype)


def _rmsnorm(x, g, tm):
    m, d = x.shape
    return pl.pallas_call(
        _rmsnorm_kernel,
        grid=(m // tm,),
        in_specs=[pl.BlockSpec((tm, d), lambda i: (i, 0)), pl.BlockSpec((1, d), lambda i: (0, 0))],
        out_specs=pl.BlockSpec((tm, d), lambda i: (i, 0)),
        out_shape=jax.ShapeDtypeStruct((m, d), BF16),
        compiler_params=_params(1),
        name="rmsnorm",
    )(x, g.reshape(1, d).astype(F32))


def _mm_kernel(x_ref, w_ref, o_ref):
    o_ref[...] = jnp.dot(x_ref[...], w_ref[...].astype(BF16), preferred_element_type=F32).astype(o_ref.dtype)


def _matmul(x, w, tm, tn, out_dtype):
    m, k = x.shape
    n = w.shape[1]
    return pl.pallas_call(
        _mm_kernel,
        grid=(m // tm, n // tn),
        in_specs=[pl.BlockSpec((tm, k), lambda i, j: (i, 0)), pl.BlockSpec((k, tn), lambda i, j: (0, j))],
        out_specs=pl.BlockSpec((tm, tn), lambda i, j: (i, j)),
        out_shape=jax.ShapeDtypeStruct((m, n), out_dtype),
        compiler_params=_params(2),
        name="matmul",
    )(x, w)


def _mm_res_kernel(x_ref, w_ref, r_ref, o_ref):
    o_ref[...] = r_ref[...] + jnp.dot(x_ref[...], w_ref[...], preferred_element_type=F32)


def _matmul_residual(x, w, r, tm, tn):
    m, k = x.shape
    n = w.shape[1]
    return pl.pallas_call(
        _mm_res_kernel,
        grid=(m // tm, n // tn),
        in_specs=[
            pl.BlockSpec((tm, k), lambda i, j: (i, 0)),
            pl.BlockSpec((k, tn), lambda i, j: (0, j)),
            pl.BlockSpec((tm, tn), lambda i, j: (i, j)),
        ],
        out_specs=pl.BlockSpec((tm, tn), lambda i, j: (i, j)),
        out_shape=jax.ShapeDtypeStruct((m, n), F32),
        compiler_params=_params(2),
        name="matmul_residual",
    )(x, w, r)


def _mm2_res_kernel(a_ref, b_ref, wa_ref, wb_ref, r_ref, o_ref):
    acc = jnp.dot(a_ref[...], wa_ref[...].astype(BF16), preferred_element_type=F32)
    acc = acc + jnp.dot(b_ref[...], wb_ref[...].astype(BF16), preferred_element_type=F32)
    o_ref[...] = r_ref[...] + acc


def _out_projection(a, b, w, r, tm, tn):
    m, ka = a.shape
    kb = b.shape[1]
    assert ka == kb and w.shape[0] == ka + kb
    n = w.shape[1]
    return pl.pallas_call(
        _mm2_res_kernel,
        grid=(m // tm, n // tn),
        in_specs=[
            pl.BlockSpec((tm, ka), lambda i, j: (i, 0)),
            pl.BlockSpec((tm, kb), lambda i, j: (i, 0)),
            pl.BlockSpec((ka, tn), lambda i, j: (0, j)),
            pl.BlockSpec((kb, tn), lambda i, j: (1, j)),
            pl.BlockSpec((tm, tn), lambda i, j: (i, j)),
        ],
        out_specs=pl.BlockSpec((tm, tn), lambda i, j: (i, j)),
        out_shape=jax.ShapeDtypeStruct((m, n), F32),
        compiler_params=_params(2),
        name="out_projection",
    )(a, b, w, w, r)


def _swiglu_kernel(x_ref, wg_ref, wu_ref, o_ref):
    x = x_ref[...]
    g = jnp.dot(x, wg_ref[...].astype(BF16), preferred_element_type=F32)
    u = jnp.dot(x, wu_ref[...].astype(BF16), preferred_element_type=F32)
    o_ref[...] = (jax.nn.silu(g) * u).astype(o_ref.dtype)


def _swiglu_up(x, wg, wu, tm, tn):
    m, k = x.shape
    n = wg.shape[1]
    return pl.pallas_call(
        _swiglu_kernel,
        grid=(m // tm, n // tn),
        in_specs=[
            pl.BlockSpec((tm, k), lambda i, j: (i, 0)),
            pl.BlockSpec((k, tn), lambda i, j: (0, j)),
            pl.BlockSpec((k, tn), lambda i, j: (0, j)),
        ],
        out_specs=pl.BlockSpec((tm, tn), lambda i, j: (i, j)),
        out_shape=jax.ShapeDtypeStruct((m, n), BF16),
        compiler_params=_params(2),
        name="swiglu_up",
    )(x, wg, wu)


def _rope_tables(seq):
    half = ROT_DIM // 2
    pos = jnp.arange(seq, dtype=F32)
    inv = ROPE_THETA ** (-jnp.arange(0, ROT_DIM, 2, dtype=F32) / ROT_DIM)
    ang = pos[:, None] * inv[None, :]
    cos, sin = jnp.cos(ang), jnp.sin(ang)
    rest = HEAD_DIM - ROT_DIM
    c = jnp.concatenate([cos, cos, jnp.ones((seq, rest), F32)], axis=-1)
    sa = jnp.concatenate([-sin, jnp.zeros((seq, HEAD_DIM - half), F32)], axis=-1)
    sb = jnp.concatenate([jnp.zeros((seq, half), F32), sin, jnp.zeros((seq, rest), F32)], axis=-1)
    return c, sa, sb


def _norm_rope(x, g, c, sa, sb):
    half = ROT_DIM // 2
    ms = jnp.mean(x * x, axis=-1, keepdims=True)
    y = x * lax.rsqrt(ms + NORM_EPS) * g
    up = pltpu.roll(y, HEAD_DIM - half, 1)
    down = pltpu.roll(y, half, 1)
    return y * c + up * sa + down * sb


def _causal_mask(s):
    key = lax.broadcasted_iota(jnp.int32, s.shape, 0)
    qry = lax.broadcasted_iota(jnp.int32, s.shape, 1)
    return jnp.where(key <= qry, s, MASKED)


def _prefix_softmax(blocks):
    m = jnp.max(blocks[0], axis=0, keepdims=True)
    for sj in blocks[1:]:
        m = jnp.maximum(m, jnp.max(sj, axis=0, keepdims=True))
    ps, l = [], None
    for sj in blocks:
        pj = jnp.exp2(sj - m)
        lj = jnp.sum(pj, axis=0, keepdims=True)
        l = lj if l is None else l + lj
        ps.append(pj)
    return ps, l


def _split_blocks(s, n):
    return [s[j * MOBA_BLOCK:(j + 1) * MOBA_BLOCK, :] for j in range(n)]


def _moba_kernel(q_ref, k_ref, v_ref, gq_ref, gk_ref, c_ref, sa_ref, sb_ref, o_ref,
                 kn_ref, vt_ref, kmean_ref):
    blk = MOBA_BLOCK
    nb = q_ref.shape[1] // blk

    for n in range(nb):
        rows = slice(n * blk, (n + 1) * blk)
        kn = _norm_rope(k_ref[0, rows, :].astype(F32), gk_ref[...],
                        c_ref[rows, :], sa_ref[rows, :], sb_ref[rows, :])
        kn_ref[rows, :] = kn.astype(BF16)
        kmean_ref[n:n + 1, :] = jnp.mean(kn, axis=0, keepdims=True)
        vt_ref[:, rows] = v_ref[0, rows, :].astype(F32).T.astype(BF16)

    def masked_scores(qi):
        rows = slice(qi * blk, (qi + 1) * blk)
        nk = (qi + 1) * blk
        qn = _norm_rope(q_ref[0, rows, :].astype(F32), gq_ref[...] * EXP2_SCALE,
                        c_ref[rows, :], sa_ref[rows, :], sb_ref[rows, :])
        s = lax.dot_general(kn_ref[0:nk, :], qn.astype(BF16), NT_DIMS, preferred_element_type=F32)
        blocks = _split_blocks(s, qi + 1)
        blocks[qi] = _causal_mask(blocks[qi])

        if qi > MOBA_TOPK:
            gate = lax.dot_general(kmean_ref[...], qn, NT_DIMS, preferred_element_type=F32,
                                   precision=lax.Precision.HIGHEST)
            n_iota = lax.broadcasted_iota(jnp.int32, gate.shape, 0)
            beaten = jnp.zeros(gate.shape, jnp.int32)
            for mth in range(qi):
                gm = gate[mth:mth + 1, :]
                wins = (gm > gate) | ((gm == gate) & (mth < n_iota))
                beaten = beaten + wins.astype(jnp.int32)
            bias = jnp.where(beaten < MOBA_TOPK, 0.0, MASKED)
            for j in range(qi):
                blocks[j] = blocks[j] + bias[j:j + 1, :]
        return blocks

    def softmax(blocks):
        ps, l = _prefix_softmax(blocks)
        return jnp.concatenate([pj.astype(BF16) for pj in ps], axis=0), l

    def weighted_values(qi, p, l):
        rows = slice(qi * blk, (qi + 1) * blk)
        acc = jnp.dot(vt_ref[:, 0:(qi + 1) * blk], p, preferred_element_type=F32)
        o_ref[0, rows, :] = (acc / l).T.astype(o_ref.dtype)

    ahead = masked_scores(0)
    for qi in range(nb):
        blocks = ahead
        if qi + 1 < nb:
            ahead = masked_scores(qi + 1)
        weighted_values(qi, *softmax(blocks))


def _moba_attention(proj, gq, gk, tables):
    b, s, _ = proj.shape
    nb = s // MOBA_BLOCK
    c, sa, sb = tables
    head = lambda off: pl.BlockSpec((1, s, HEAD_DIM), lambda bi, h: (bi, 0, off + h))
    full = lambda shape: pl.BlockSpec(shape, lambda bi, h: (0,) * len(shape))
    return pl.pallas_call(
        _moba_kernel,
        grid=(b, MOBA_HEADS),
        in_specs=[head(0), head(MOBA_HEADS), head(2 * MOBA_HEADS),
                  full((1, HEAD_DIM)), full((1, HEAD_DIM)),
                  full((s, HEAD_DIM)), full((s, HEAD_DIM)), full((s, HEAD_DIM))],
        out_specs=pl.BlockSpec((1, s, HEAD_DIM), lambda bi, h: (bi, 0, h)),
        out_shape=jax.ShapeDtypeStruct((b, s, MOBA_WIDTH), BF16),
        scratch_shapes=[
            pltpu.VMEM((s, HEAD_DIM), BF16),
            pltpu.VMEM((HEAD_DIM, s), BF16),
            pltpu.VMEM((nb, HEAD_DIM), F32),
        ],
        compiler_params=_params(2),
        name="moba_attention",
    )(proj, proj, proj, gq.reshape(1, -1).astype(F32), gk.reshape(1, -1).astype(F32), c, sa, sb)


def _diff_kernel(lam_init, q_ref, k_ref, v_ref, gq_ref, gk_ref, lq1_ref, lk1_ref, lq2_ref, lk2_ref,
                 gs_ref, c_ref, sa_ref, sb_ref, o_ref, k1_ref, k2_ref, vt_ref):
    blk = MOBA_BLOCK
    nb = q_ref.shape[1] // blk
    hd = HEAD_DIM

    lam = (jnp.exp(jnp.sum(lq1_ref[...] * lk1_ref[...], axis=-1, keepdims=True))
           - jnp.exp(jnp.sum(lq2_ref[...] * lk2_ref[...], axis=-1, keepdims=True))
           + lam_init)

    def halves(ref, rows, g):
        tabs = (c_ref[rows, :], sa_ref[rows, :], sb_ref[rows, :])
        x1 = _norm_rope(ref[0, rows, 0:hd].astype(F32), g, *tabs)
        x2 = _norm_rope(ref[0, rows, hd:2 * hd].astype(F32), g, *tabs)
        return x1.astype(BF16), x2.astype(BF16)

    for n in range(nb):
        rows = slice(n * blk, (n + 1) * blk)
        k1_ref[rows, :], k2_ref[rows, :] = halves(k_ref, rows, gk_ref[...])
        vt_ref[:, rows] = v_ref[0, rows, :].astype(F32).T.astype(BF16)

    def masked_scores(qi):
        rows = slice(qi * blk, (qi + 1) * blk)
        nk = (qi + 1) * blk
        out = []
        for kref, qh in zip((k1_ref, k2_ref), halves(q_ref, rows, gq_ref[...] * EXP2_SCALE)):
            s = lax.dot_general(kref[0:nk, :], qh, NT_DIMS, preferred_element_type=F32)
            blocks = _split_blocks(s, qi + 1)
            blocks[qi] = _causal_mask(blocks[qi])
            out.append(blocks)
        return out

    def attend(qi, blocks1, blocks2):
        rows = slice(qi * blk, (qi + 1) * blk)
        nk = (qi + 1) * blk
        p1, l1 = _prefix_softmax(blocks1)
        p2, l2 = _prefix_softmax(blocks2)
        r1 = 1.0 / l1
        r2 = lam / l2
        w = jnp.concatenate([(a * r1 - b * r2).astype(BF16) for a, b in zip(p1, p2)], axis=0)
        o = jnp.dot(vt_ref[:, 0:nk], w, preferred_element_type=F32)
        ms = jnp.mean(o * o, axis=0, keepdims=True)
        on = (o * lax.rsqrt(ms + NORM_EPS)).T * gs_ref[...]
        o_ref[0, rows, :] = (on * (1.0 - lam_init)).astype(o_ref.dtype)

    for qi in range(nb):
        attend(qi, *masked_scores(qi))


def _diff_attention(proj, gq, gk, lq1, lk1, lq2, lk2, gs, lam_init, tables):
    b, s, _ = proj.shape
    c, sa, sb = tables
    qoff = 3 * MOBA_WIDTH // DIFF_V_DIM
    head = lambda off: pl.BlockSpec((1, s, DIFF_V_DIM), lambda bi, h: (bi, 0, off + h))
    full = lambda shape: pl.BlockSpec(shape, lambda bi, h: (0,) * len(shape))
    row = lambda v: v.reshape(1, -1).astype(F32)
    return pl.pallas_call(
        functools.partial(_diff_kernel, lam_init),
        grid=(b, DIFF_HEADS),
        in_specs=[head(qoff), head(qoff + DIFF_HEADS), head(qoff + 2 * DIFF_HEADS),
                  full((1, HEAD_DIM)), full((1, HEAD_DIM)),
                  full((1, HEAD_DIM)), full((1, HEAD_DIM)), full((1, HEAD_DIM)), full((1, HEAD_DIM)),
                  full((1, DIFF_V_DIM)),
                  full((s, HEAD_DIM)), full((s, HEAD_DIM)), full((s, HEAD_DIM))],
        out_specs=pl.BlockSpec((1, s, DIFF_V_DIM), lambda bi, h: (bi, 0, h)),
        out_shape=jax.ShapeDtypeStruct((b, s, DIFF_HEADS * DIFF_V_DIM), BF16),
        scratch_shapes=[
            pltpu.VMEM((s, HEAD_DIM), BF16),
            pltpu.VMEM((s, HEAD_DIM), BF16),
            pltpu.VMEM((DIFF_V_DIM, s), BF16),
        ],
        compiler_params=_params(2),
        name="diff_attention",
    )(proj, proj, proj, row(gq), row(gk), row(lq1), row(lk1), row(lq2), row(lk2), row(gs), c, sa, sb)


def _mem_kv_kernel(m_ref, wk_ref, wv_ref, gk_ref, k_ref, v_ref):
    x = m_ref[...]
    k = jnp.dot(x, wk_ref[...].astype(BF16), preferred_element_type=F32)
    for h in range(MEM_HEADS):
        cols = slice(h * HEAD_DIM, (h + 1) * HEAD_DIM)
        kh = k[:, cols]
        ms = jnp.mean(kh * kh, axis=-1, keepdims=True)
        k_ref[:, cols] = (kh * lax.rsqrt(ms + NORM_EPS) * gk_ref[...]).astype(k_ref.dtype)
    v_ref[...] = jnp.dot(x, wv_ref[...].astype(BF16), preferred_element_type=F32).astype(v_ref.dtype)


def _mem_kv(mem_n, wk, wv, gk, tm):
    m, d = mem_n.shape
    return pl.pallas_call(
        _mem_kv_kernel,
        grid=(m // tm,),
        in_specs=[pl.BlockSpec((tm, d), lambda i: (i, 0)),
                  pl.BlockSpec((d, MEM_WIDTH), lambda i: (0, 0)),
                  pl.BlockSpec((d, MEM_WIDTH), lambda i: (0, 0)),
                  pl.BlockSpec((1, HEAD_DIM), lambda i: (0, 0))],
        out_specs=[pl.BlockSpec((tm, MEM_WIDTH), lambda i: (i, 0)),
                   pl.BlockSpec((tm, MEM_WIDTH), lambda i: (i, 0))],
        out_shape=[jax.ShapeDtypeStruct((m, MEM_WIDTH), BF16), jax.ShapeDtypeStruct((m, MEM_WIDTH), BF16)],
        compiler_params=_params(1),
        name="mem_kv",
    )(mem_n, wk, wv, gk.reshape(1, -1).astype(F32))


def _row_rmsnorm(x, g):
    ms = jnp.mean(x * x, axis=-1, keepdims=True)
    return x * lax.rsqrt(ms + NORM_EPS) * g


def _cross_kernel(h_ref, k_ref, v_ref, wq_ref, wo_ref, gq_ref, gin_ref, gout_ref, o_ref, on_ref):
    hn = _row_rmsnorm(h_ref[...], gin_ref[...]).astype(BF16)
    q = jnp.dot(hn, wq_ref[...], preferred_element_type=F32)
    heads = []
    for h in range(MEM_HEADS):
        cols = slice(h * HEAD_DIM, (h + 1) * HEAD_DIM)
        qh = q[:, cols]
        ms = jnp.mean(qh * qh, axis=-1, keepdims=True)
        qh = (qh * lax.rsqrt(ms + NORM_EPS) * gq_ref[...]).astype(BF16)
        s = lax.dot_general(qh, k_ref[0, :, cols], NT_DIMS, preferred_element_type=F32) * ATTN_SCALE
        e = jnp.exp(s - jnp.max(s, axis=-1, keepdims=True))
        p = (e / jnp.sum(e, axis=-1, keepdims=True)).astype(BF16)
        heads.append(jnp.dot(p, v_ref[0, :, cols], preferred_element_type=F32).astype(BF16))
    o = jnp.concatenate(heads, axis=-1)
    out = h_ref[...] + jnp.dot(o, wo_ref[...], preferred_element_type=F32)
    o_ref[...] = out
    on_ref[...] = _row_rmsnorm(out, gout_ref[...]).astype(on_ref.dtype)


def _cross_attention(h, k, v, wq, wo, gq, g_in, g_out, seq, tm):
    m, d = h.shape
    mem_len = k.shape[1]
    per_batch = seq // tm
    row = lambda g: g.reshape(1, -1).astype(F32)
    return pl.pallas_call(
        _cross_kernel,
        grid=(m // tm,),
        in_specs=[pl.BlockSpec((tm, d), lambda i: (i, 0)),
                  pl.BlockSpec((1, mem_len, MEM_WIDTH), lambda i: (i // per_batch, 0, 0)),
                  pl.BlockSpec((1, mem_len, MEM_WIDTH), lambda i: (i // per_batch, 0, 0)),
                  pl.BlockSpec((d, MEM_WIDTH), lambda i: (0, 0)),
                  pl.BlockSpec((MEM_WIDTH, d), lambda i: (0, 0)),
                  pl.BlockSpec((1, HEAD_DIM), lambda i: (0, 0)),
                  pl.BlockSpec((1, d), lambda i: (0, 0)),
                  pl.BlockSpec((1, d), lambda i: (0, 0))],
        out_specs=[pl.BlockSpec((tm, d), lambda i: (i, 0)), pl.BlockSpec((tm, d), lambda i: (i, 0))],
        out_shape=[jax.ShapeDtypeStruct((m, d), F32), jax.ShapeDtypeStruct((m, d), BF16)],
        compiler_params=_params(1),
        name="cross_attention",
    )(h, k, v, wq, wo, row(gq), row(g_in), row(g_out))


def _layer(h, mem_n_src, l, p):
    b, s, d = h.shape
    m = b * s
    lam_init = 0.8 - 0.6 * math.exp(-0.3 * l)
    tables = _rope_tables(s)
    bf = lambda w: w.astype(BF16)
    h2d = h.reshape(m, d)

    xn = _rmsnorm(h2d, p["norm_mix_g"][l], 256)
    proj = _matmul(xn, p["w_in"][l], 1024, 512, BF16).reshape(b, s, IN_WIDTH)
    out_a = _moba_attention(proj, p["q_norm_a"][l], p["k_norm_a"][l], tables)
    out_b = _diff_attention(proj, p["q_norm_b"][l], p["k_norm_b"][l], p["lam_q1"][l], p["lam_k1"][l],
                            p["lam_q2"][l], p["lam_k2"][l], p["diff_subln_g"][l], lam_init, tables)
    h2d = _out_projection(out_a.reshape(m, -1), out_b.reshape(m, -1), p["w_out"][l], h2d, 1024, 512)

    mem_len = mem_n_src.shape[1]
    mem_n = _rmsnorm(mem_n_src.reshape(b * mem_len, d), p["norm_mem_g"][l], 256)
    mk, mv = _mem_kv(mem_n, p["w_mk"][l], p["w_mv"][l], p["k_norm_m"][l], 256)
    h2d, hn = _cross_attention(h2d, mk.reshape(b, mem_len, -1), mv.reshape(b, mem_len, -1),
                               bf(p["w_mq"][l]), bf(p["w_mo"][l]), p["q_norm_m"][l],
                               p["norm_cross_g"][l], p["norm_ffn_g"][l], s, 256)

    act = _swiglu_up(hn, p["w_gate"][l], p["w_up"][l], 1024, 256)
    h2d = _matmul_residual(act, bf(p["w_down"][l]), h2d, 512, 512)
    return h2d.reshape(b, s, d)


def kernel(x, mem, norm_mix_g, w_in, q_norm_a, k_norm_a, q_norm_b, k_norm_b, lam_q1, lam_k1, lam_q2, lam_k2, diff_subln_g, w_out, norm_cross_g, norm_mem_g, w_mq, w_mk, w_mv, w_mo, q_norm_m, k_norm_m, norm_ffn_g, w_gate, w_up, w_down):
    p = dict(norm_mix_g=norm_mix_g, w_in=w_in, q_norm_a=q_norm_a, k_norm_a=k_norm_a, q_norm_b=q_norm_b,
             k_norm_b=k_norm_b, lam_q1=lam_q1, lam_k1=lam_k1, lam_q2=lam_q2, lam_k2=lam_k2,
             diff_subln_g=diff_subln_g, w_out=w_out, norm_cross_g=norm_cross_g, norm_mem_g=norm_mem_g,
             w_mq=w_mq, w_mk=w_mk, w_mv=w_mv, w_mo=w_mo, q_norm_m=q_norm_m, k_norm_m=k_norm_m,
             norm_ffn_g=norm_ffn_g, w_gate=w_gate, w_up=w_up, w_down=w_down)
    h = x
    for l in range(w_in.shape[0]):
        h = _layer(h, mem, l, p)
    return h
```

```python
import functools
import math

import jax
import jax.numpy as jnp
from jax import lax
from jax.experimental import pallas as pl
from jax.experimental.pallas import tpu as pltpu

F32 = jnp.float32
BF16 = jnp.bfloat16

D_MODEL = 4096
HEAD_DIM = 128
MOBA_HEADS = 16
DIFF_HEADS = 8
MOBA_WIDTH = MOBA_HEADS * HEAD_DIM
DIFF_V_DIM = 2 * HEAD_DIM
IN_WIDTH = 12288
MOBA_BLOCK = 256
MOBA_TOPK = 3
ROPE_THETA = 500000.0
ROT_DIM = HEAD_DIM // 4
MEM_HEADS = 4
MEM_WIDTH = MEM_HEADS * HEAD_DIM
NORM_EPS = 1e-6
ATTN_SCALE = HEAD_DIM ** -0.5
MASKED = -1e30
EXP2_SCALE = ATTN_SCALE * math.log2(math.e)

V7X_VMEM_BYTES = 64 * 1024 * 1024
VMEM_LIMIT = V7X_VMEM_BYTES - 8 * 1024 * 1024

NT_DIMS = (((1,), (1,)), ((), ()))


def _params(n_grid):
    return pltpu.CompilerParams(
        dimension_semantics=("parallel",) * (n_grid - 1) + ("arbitrary",),
        vmem_limit_bytes=VMEM_LIMIT,
    )


def _rmsnorm_kernel(x_ref, g_ref, o_ref):
    x = x_ref[...].astype(F32)
    ms = jnp.mean(x * x, axis=-1, keepdims=True)
    o_ref[...] = (x * lax.rsqrt(ms + NORM_EPS) * g_ref[...]).astype(o_ref.dtype)


def _rmsnorm(x, g, tm):
    m, d = x.shape
    return pl.pallas_call(
        _rmsnorm_kernel,
        grid=(m // tm,),
        in_specs=[pl.BlockSpec((tm, d), lambda i: (i, 0)), pl.BlockSpec((1, d), lambda i: (0, 0))],
        out_specs=pl.BlockSpec((tm, d), lambda i: (i, 0)),
        out_shape=jax.ShapeDtypeStruct((m, d), BF16),
        compiler_params=_params(1),
        name="rmsnorm",
    )(x, g.reshape(1, d).astype(F32))


def _mm_res_kernel(x_ref, w_ref, r_ref, o_ref):
    o_ref[...] = r_ref[...] + jnp.dot(x_ref[...], w_ref[...], preferred_element_type=F32)


def _matmul_residual(x, w, r, tm, tn):
    m, k = x.shape
    n = w.shape[1]
    return pl.pallas_call(
        _mm_res_kernel,
        grid=(m // tm, n // tn),
        in_specs=[
            pl.BlockSpec((tm, k), lambda i, j: (i, 0)),
            pl.BlockSpec((k, tn), lambda i, j: (0, j)),
            pl.BlockSpec((tm, tn), lambda i, j: (i, j)),
        ],
        out_specs=pl.BlockSpec((tm, tn), lambda i, j: (i, j)),
        out_shape=jax.ShapeDtypeStruct((m, n), F32),
        compiler_params=_params(2),
        name="matmul_residual",
    )(x, w, r)


def _mm2_res_kernel(a_ref, b_ref, wa_ref, wb_ref, r_ref, o_ref):
    acc = jnp.dot(a_ref[...], wa_ref[...].astype(BF16), preferred_element_type=F32)
    acc = acc + jnp.dot(b_ref[...], wb_ref[...].astype(BF16), preferred_element_type=F32)
    o_ref[...] = r_ref[...] + acc


def _out_projection(a, b, w, r, tm, tn):
    m, ka = a.shape
    kb = b.shape[1]
    assert ka == kb and w.shape[0] == ka + kb
    n = w.shape[1]
    return pl.pallas_call(
        _mm2_res_kernel,
        grid=(m // tm, n // tn),
        in_specs=[
            pl.BlockSpec((tm, ka), lambda i, j: (i, 0)),
            pl.BlockSpec((tm, kb), lambda i, j: (i, 0)),
            pl.BlockSpec((ka, tn), lambda i, j: (0, j)),
            pl.BlockSpec((kb, tn), lambda i, j: (1, j)),
            pl.BlockSpec((tm, tn), lambda i, j: (i, j)),
        ],
        out_specs=pl.BlockSpec((tm, tn), lambda i, j: (i, j)),
        out_shape=jax.ShapeDtypeStruct((m, n), F32),
        compiler_params=_params(2),
        name="out_projection",
    )(a, b, w, w, r)


def _swiglu_kernel(x_ref, wg_ref, wu_ref, o_ref):
    x = x_ref[...]
    g = jnp.dot(x, wg_ref[...].astype(BF16), preferred_element_type=F32)
    u = jnp.dot(x, wu_ref[...].astype(BF16), preferred_element_type=F32)
    o_ref[...] = (jax.nn.silu(g) * u).astype(o_ref.dtype)


def _swiglu_up(x, wg, wu, tm, tn):
    m, k = x.shape
    n = wg.shape[1]
    return pl.pallas_call(
        _swiglu_kernel,
        grid=(m // tm, n // tn),
        in_specs=[
            pl.BlockSpec((tm, k), lambda i, j: (i, 0)),
            pl.BlockSpec((k, tn), lambda i, j: (0, j)),
            pl.BlockSpec((k, tn), lambda i, j: (0, j)),
        ],
        out_specs=pl.BlockSpec((tm, tn), lambda i, j: (i, j)),
        out_shape=jax.ShapeDtypeStruct((m, n), BF16),
        compiler_params=_params(2),
        name="swiglu_up",
    )(x, wg, wu)


def _rope_tables(seq):
    half = ROT_DIM // 2
    pos = jnp.arange(seq, dtype=F32)
    inv = ROPE_THETA ** (-jnp.arange(0, ROT_DIM, 2, dtype=F32) / ROT_DIM)
    ang = pos[:, None] * inv[None, :]
    cos, sin = jnp.cos(ang), jnp.sin(ang)
    rest = HEAD_DIM - ROT_DIM
    c = jnp.concatenate([cos, cos, jnp.ones((seq, rest), F32)], axis=-1)
    sa = jnp.concatenate([-sin, jnp.zeros((seq, HEAD_DIM - half), F32)], axis=-1)
    sb = jnp.concatenate([jnp.zeros((seq, half), F32), sin, jnp.zeros((seq, rest), F32)], axis=-1)
    return (jnp.stack([c, jnp.ones_like(c)]), jnp.stack([sa, jnp.zeros_like(sa)]),
            jnp.stack([sb, jnp.zeros_like(sb)]))


def _head_gains(p, l):
    ones = jnp.ones((HEAD_DIM,), F32)
    rows = [p["q_norm_a"][l].astype(F32) * EXP2_SCALE, p["k_norm_a"][l].astype(F32), ones,
            p["q_norm_b"][l].astype(F32) * EXP2_SCALE, p["k_norm_b"][l].astype(F32), ones]
    return jnp.stack(rows).reshape(6, 1, HEAD_DIM)


def _norm_rope(x, g, c, sa, sb, passthrough):
    half = ROT_DIM // 2
    ms = jnp.mean(x * x, axis=-1, keepdims=True)
    y = x * jnp.where(passthrough, 1.0, lax.rsqrt(ms + NORM_EPS)) * g
    up = pltpu.roll(y, HEAD_DIM - half, 1)
    down = pltpu.roll(y, half, 1)
    return y * c + up * sa + down * sb


def _in_proj_kernel(nj, x_ref, w_ref, g_ref, c_ref, sa_ref, sb_ref, o_ref, acc_ref):
    t = pl.program_id(0)
    tn = o_ref.shape[1]

    @pl.when(t == 0)
    def _():
        acc_ref[...] = jnp.zeros_like(acc_ref)

    prev = acc_ref[...]
    acc_ref[...] = jnp.dot(x_ref[...], w_ref[...].astype(BF16), preferred_element_type=F32)

    region = ((jnp.maximum(t - 1, 0) % nj) * tn) // MOBA_WIDTH
    is_value = region % 3 == 2
    for h in range(tn // HEAD_DIM):
        cols = slice(h * HEAD_DIM, (h + 1) * HEAD_DIM)
        out = _norm_rope(prev[:, cols], g_ref[0], c_ref[0], sa_ref[0], sb_ref[0], is_value)
        o_ref[:, cols] = out.astype(o_ref.dtype)


def _in_projection(xn, w, gains, tables, seq, tm, tn):
    m, k = xn.shape
    n = w.shape[1]
    assert MOBA_WIDTH % tn == 0 and seq % tm == 0
    nj = n // tn
    tiles = (m // tm) * nj
    mm = lambda t: jnp.minimum(t, tiles - 1)
    ep = lambda t: jnp.maximum(t - 1, 0)
    region = lambda t: ((ep(t) % nj) * tn) // MOBA_WIDTH
    table_spec = pl.BlockSpec((1, tm, HEAD_DIM),
                              lambda t: ((region(t) % 3) // 2, (ep(t) // nj) % (seq // tm), 0))
    return pl.pallas_call(
        functools.partial(_in_proj_kernel, nj),
        grid=(tiles + 1,),
        in_specs=[pl.BlockSpec((tm, k), lambda t: (mm(t) // nj, 0)),
                  pl.BlockSpec((k, tn), lambda t: (0, mm(t) % nj)),
                  pl.BlockSpec((1, 1, HEAD_DIM), lambda t: (region(t), 0, 0)),
                  table_spec, table_spec, table_spec],
        out_specs=pl.BlockSpec((tm, tn), lambda t: (ep(t) // nj, ep(t) % nj)),
        out_shape=jax.ShapeDtypeStruct((m, n), BF16),
        scratch_shapes=[pltpu.VMEM((tm, tn), F32)],
        compiler_params=_params(1),
        name="in_projection",
    )(xn, w, gains, *tables)


def _causal_mask(s):
    key = lax.broadcasted_iota(jnp.int32, s.shape, 0)
    qry = lax.broadcasted_iota(jnp.int32, s.shape, 1)
    return jnp.where(key <= qry, s, MASKED)


def _prefix_softmax(blocks):
    m = jnp.max(blocks[0], axis=0, keepdims=True)
    for sj in blocks[1:]:
        m = jnp.maximum(m, jnp.max(sj, axis=0, keepdims=True))
    ps, l = [], None
    for sj in blocks:
        pj = jnp.exp2(sj - m)
        lj = jnp.sum(pj, axis=0, keepdims=True)
        l = lj if l is None else l + lj
        ps.append(pj)
    return ps, l


def _split_blocks(s, n):
    return [s[j * MOBA_BLOCK:(j + 1) * MOBA_BLOCK, :] for j in range(n)]


def _moba_kernel(q_ref, k_ref, v_ref, o_ref, vt_ref, kmean_ref):
    blk = MOBA_BLOCK
    nb = q_ref.shape[1] // blk

    for n in range(nb):
        rows = slice(n * blk, (n + 1) * blk)
        kmean_ref[n:n + 1, :] = jnp.mean(k_ref[0, rows, :].astype(F32), axis=0, keepdims=True)
        vt_ref[:, rows] = v_ref[0, rows, :].astype(F32).T.astype(BF16)

    def masked_scores(qi):
        rows = slice(qi * blk, (qi + 1) * blk)
        nk = (qi + 1) * blk
        qb = q_ref[0, rows, :]
        s = lax.dot_general(k_ref[0, 0:nk, :], qb, NT_DIMS, preferred_element_type=F32)
        blocks = _split_blocks(s, qi + 1)
        blocks[qi] = _causal_mask(blocks[qi])

        if qi > MOBA_TOPK:
            gate = lax.dot_general(kmean_ref[...], qb.astype(F32), NT_DIMS, preferred_element_type=F32,
                                   precision=lax.Precision.HIGHEST)
            n_iota = lax.broadcasted_iota(jnp.int32, gate.shape, 0)
            beaten = jnp.zeros(gate.shape, jnp.int32)
            for mth in range(qi):
                gm = gate[mth:mth + 1, :]
                wins = (gm > gate) | ((gm == gate) & (mth < n_iota))
                beaten = beaten + wins.astype(jnp.int32)
            bias = jnp.where(beaten < MOBA_TOPK, 0.0, MASKED)
            for j in range(qi):
                blocks[j] = blocks[j] + bias[j:j + 1, :]
        return blocks

    def softmax(blocks):
        ps, l = _prefix_softmax(blocks)
        return jnp.concatenate([pj.astype(BF16) for pj in ps], axis=0), l

    def weighted_values(qi, p, l):
        rows = slice(qi * blk, (qi + 1) * blk)
        acc = jnp.dot(vt_ref[:, 0:(qi + 1) * blk], p, preferred_element_type=F32)
        o_ref[0, rows, :] = (acc / l).T.astype(o_ref.dtype)

    ahead = masked_scores(0)
    for qi in range(nb):
        blocks = ahead
        if qi + 1 < nb:
            ahead = masked_scores(qi + 1)
        weighted_values(qi, *softmax(blocks))


def _moba_attention(proj):
    b, s, _ = proj.shape
    nb = s // MOBA_BLOCK
    head = lambda off: pl.BlockSpec((1, s, HEAD_DIM), lambda bi, h: (bi, 0, off + h))
    return pl.pallas_call(
        _moba_kernel,
        grid=(b, MOBA_HEADS),
        in_specs=[head(0), head(MOBA_HEADS), head(2 * MOBA_HEADS)],
        out_specs=pl.BlockSpec((1, s, HEAD_DIM), lambda bi, h: (bi, 0, h)),
        out_shape=jax.ShapeDtypeStruct((b, s, MOBA_WIDTH), BF16),
        scratch_shapes=[
            pltpu.VMEM((HEAD_DIM, s), BF16),
            pltpu.VMEM((nb, HEAD_DIM), F32),
        ],
        compiler_params=_params(2),
        name="moba_attention",
    )(proj, proj, proj)


def _diff_kernel(lam_init, q_ref, k_ref, v_ref, lq1_ref, lk1_ref, lq2_ref, lk2_ref, gs_ref,
                 o_ref, vt_ref):
    blk = MOBA_BLOCK
    nb = q_ref.shape[1] // blk
    hd = HEAD_DIM

    lam = (jnp.exp(jnp.sum(lq1_ref[...] * lk1_ref[...], axis=-1, keepdims=True))
           - jnp.exp(jnp.sum(lq2_ref[...] * lk2_ref[...], axis=-1, keepdims=True))
           + lam_init)

    for n in range(nb):
        rows = slice(n * blk, (n + 1) * blk)
        vt_ref[:, rows] = v_ref[0, rows, :].astype(F32).T.astype(BF16)

    def masked_scores(qi):
        rows = slice(qi * blk, (qi + 1) * blk)
        nk = (qi + 1) * blk
        out = []
        for half in range(2):
            cols = slice(half * hd, (half + 1) * hd)
            s = lax.dot_general(k_ref[0, 0:nk, cols], q_ref[0, rows, cols], NT_DIMS,
                                preferred_element_type=F32)
            blocks = _split_blocks(s, qi + 1)
            blocks[qi] = _causal_mask(blocks[qi])
            out.append(blocks)
        return out

    def attend(qi, blocks1, blocks2):
        rows = slice(qi * blk, (qi + 1) * blk)
        nk = (qi + 1) * blk
        p1, l1 = _prefix_softmax(blocks1)
        p2, l2 = _prefix_softmax(blocks2)
        r1 = 1.0 / l1
        r2 = lam / l2
        w = jnp.concatenate([(a * r1 - b * r2).astype(BF16) for a, b in zip(p1, p2)], axis=0)
        o = jnp.dot(vt_ref[:, 0:nk], w, preferred_element_type=F32)
        ms = jnp.mean(o * o, axis=0, keepdims=True)
        on = (o * lax.rsqrt(ms + NORM_EPS)).T * gs_ref[...]
        o_ref[0, rows, :] = (on * (1.0 - lam_init)).astype(o_ref.dtype)

    for qi in range(nb):
        attend(qi, *masked_scores(qi))


def _diff_attention(proj, lq1, lk1, lq2, lk2, gs, lam_init):
    b, s, _ = proj.shape
    qoff = 3 * MOBA_WIDTH // DIFF_V_DIM
    head = lambda off: pl.BlockSpec((1, s, DIFF_V_DIM), lambda bi, h: (bi, 0, off + h))
    full = lambda shape: pl.BlockSpec(shape, lambda bi, h: (0,) * len(shape))
    row = lambda v: v.reshape(1, -1).astype(F32)
    return pl.pallas_call(
        functools.partial(_diff_kernel, lam_init),
        grid=(b, DIFF_HEADS),
        in_specs=[head(qoff), head(qoff + DIFF_HEADS), head(qoff + 2 * DIFF_HEADS),
                  full((1, HEAD_DIM)), full((1, HEAD_DIM)), full((1, HEAD_DIM)), full((1, HEAD_DIM)),
                  full((1, DIFF_V_DIM))],
        out_specs=pl.BlockSpec((1, s, DIFF_V_DIM), lambda bi, h: (bi, 0, h)),
        out_shape=jax.ShapeDtypeStruct((b, s, DIFF_HEADS * DIFF_V_DIM), BF16),
        scratch_shapes=[pltpu.VMEM((DIFF_V_DIM, s), BF16)],
        compiler_params=_params(2),
        name="diff_attention",
    )(proj, proj, proj, row(lq1), row(lk1), row(lq2), row(lk2), row(gs))


def _mem_kv_kernel(m_ref, wk_ref, wv_ref, gk_ref, k_ref, v_ref):
    x = m_ref[...]
    k = jnp.dot(x, wk_ref[...].astype(BF16), preferred_element_type=F32)
    for h in range(MEM_HEADS):
        cols = slice(h * HEAD_DIM, (h + 1) * HEAD_DIM)
        kh = k[:, cols]
        ms = jnp.mean(kh * kh, axis=-1, keepdims=True)
        k_ref[:, cols] = (kh * lax.rsqrt(ms + NORM_EPS) * gk_ref[...]).astype(k_ref.dtype)
    v_ref[...] = jnp.dot(x, wv_ref[...].astype(BF16), preferred_element_type=F32).astype(v_ref.dtype)


def _mem_kv(mem_n, wk, wv, gk, tm):
    m, d = mem_n.shape
    return pl.pallas_call(
        _mem_kv_kernel,
        grid=(m // tm,),
        in_specs=[pl.BlockSpec((tm, d), lambda i: (i, 0)),
                  pl.BlockSpec((d, MEM_WIDTH), lambda i: (0, 0)),
                  pl.BlockSpec((d, MEM_WIDTH), lambda i: (0, 0)),
                  pl.BlockSpec((1, HEAD_DIM), lambda i: (0, 0))],
        out_specs=[pl.BlockSpec((tm, MEM_WIDTH), lambda i: (i, 0)),
                   pl.BlockSpec((tm, MEM_WIDTH), lambda i: (i, 0))],
        out_shape=[jax.ShapeDtypeStruct((m, MEM_WIDTH), BF16), jax.ShapeDtypeStruct((m, MEM_WIDTH), BF16)],
        compiler_params=_params(1),
        name="mem_kv",
    )(mem_n, wk, wv, gk.reshape(1, -1).astype(F32))


def _row_rmsnorm(x, g):
    ms = jnp.mean(x * x, axis=-1, keepdims=True)
    return x * lax.rsqrt(ms + NORM_EPS) * g


def _cross_kernel(h_ref, k_ref, v_ref, wq_ref, wo_ref, gq_ref, gin_ref, gout_ref, o_ref, on_ref):
    hn = _row_rmsnorm(h_ref[...], gin_ref[...]).astype(BF16)
    q = jnp.dot(hn, wq_ref[...], preferred_element_type=F32)
    heads = []
    for h in range(MEM_HEADS):
        cols = slice(h * HEAD_DIM, (h + 1) * HEAD_DIM)
        qh = q[:, cols]
        ms = jnp.mean(qh * qh, axis=-1, keepdims=True)
        qh = (qh * lax.rsqrt(ms + NORM_EPS) * gq_ref[...]).astype(BF16)
        s = lax.dot_general(qh, k_ref[0, :, cols], NT_DIMS, preferred_element_type=F32) * ATTN_SCALE
        e = jnp.exp(s - jnp.max(s, axis=-1, keepdims=True))
        p = (e / jnp.sum(e, axis=-1, keepdims=True)).astype(BF16)
        heads.append(jnp.dot(p, v_ref[0, :, cols], preferred_element_type=F32).astype(BF16))
    o = jnp.concatenate(heads, axis=-1)
    out = h_ref[...] + jnp.dot(o, wo_ref[...], preferred_element_type=F32)
    o_ref[...] = out
    on_ref[...] = _row_rmsnorm(out, gout_ref[...]).astype(on_ref.dtype)


def _cross_attention(h, k, v, wq, wo, gq, g_in, g_out, seq, tm):
    m, d = h.shape
    mem_len = k.shape[1]
    per_batch = seq // tm
    row = lambda g: g.reshape(1, -1).astype(F32)
    return pl.pallas_call(
        _cross_kernel,
        grid=(m // tm,),
        in_specs=[pl.BlockSpec((tm, d), lambda i: (i, 0)),
                  pl.BlockSpec((1, mem_len, MEM_WIDTH), lambda i: (i // per_batch, 0, 0)),
                  pl.BlockSpec((1, mem_len, MEM_WIDTH), lambda i: (i // per_batch, 0, 0)),
                  pl.BlockSpec((d, MEM_WIDTH), lambda i: (0, 0)),
                  pl.BlockSpec((MEM_WIDTH, d), lambda i: (0, 0)),
                  pl.BlockSpec((1, HEAD_DIM), lambda i: (0, 0)),
                  pl.BlockSpec((1, d), lambda i: (0, 0)),
                  pl.BlockSpec((1, d), lambda i: (0, 0))],
        out_specs=[pl.BlockSpec((tm, d), lambda i: (i, 0)), pl.BlockSpec((tm, d), lambda i: (i, 0))],
        out_shape=[jax.ShapeDtypeStruct((m, d), F32), jax.ShapeDtypeStruct((m, d), BF16)],
        compiler_params=_params(1),
        name="cross_attention",
    )(h, k, v, wq, wo, row(gq), row(g_in), row(g_out))


def _layer(h, mem_n_src, l, p):
    b, s, d = h.shape
    m = b * s
    lam_init = 0.8 - 0.6 * math.exp(-0.3 * l)
    tables = _rope_tables(s)
    bf = lambda w: w.astype(BF16)
    h2d = h.reshape(m, d)

    xn = _rmsnorm(h2d, p["norm_mix_g"][l], 256)
    proj = _in_projection(xn, p["w_in"][l], _head_gains(p, l), tables, s, 1024, 512).reshape(b, s, IN_WIDTH)
    out_a = _moba_attention(proj)
    out_b = _diff_attention(proj, p["lam_q1"][l], p["lam_k1"][l], p["lam_q2"][l], p["lam_k2"][l],
                            p["diff_subln_g"][l], lam_init)
    h2d = _out_projection(out_a.reshape(m, -1), out_b.reshape(m, -1), p["w_out"][l], h2d, 1024, 512)

    mem_len = mem_n_src.shape[1]
    mem_n = _rmsnorm(mem_n_src.reshape(b * mem_len, d), p["norm_mem_g"][l], 256)
    mk, mv = _mem_kv(mem_n, p["w_mk"][l], p["w_mv"][l], p["k_norm_m"][l], 256)
    h2d, hn = _cross_attention(h2d, mk.reshape(b, mem_len, -1), mv.reshape(b, mem_len, -1),
                               bf(p["w_mq"][l]), bf(p["w_mo"][l]), p["q_norm_m"][l],
                               p["norm_cross_g"][l], p["norm_ffn_g"][l], s, 256)

    act = _swiglu_up(hn, p["w_gate"][l], p["w_up"][l], 1024, 256)
    h2d = _matmul_residual(act, bf(p["w_down"][l]), h2d, 512, 512)
    return h2d.reshape(b, s, d)


def kernel(x, mem, norm_mix_g, w_in, q_norm_a, k_norm_a, q_norm_b, k_norm_b, lam_q1, lam_k1, lam_q2, lam_k2, diff_subln_g, w_out, norm_cross_g, norm_mem_g, w_mq, w_mk, w_mv, w_mo, q_norm_m, k_norm_m, norm_ffn_g, w_gate, w_up, w_down):
    p = dict(norm_mix_g=norm_mix_g, w_in=w_in, q_norm_a=q_norm_a, k_norm_a=k_norm_a, q_norm_b=q_norm_b,
             k_norm_b=k_norm_b, lam_q1=lam_q1, lam_k1=lam_k1, lam_q2=lam_q2, lam_k2=lam_k2,
             diff_subln_g=diff_subln_g, w_out=w_out, norm_cross_g=norm_cross_g, norm_mem_g=norm_mem_g,
             w_mq=w_mq, w_mk=w_mk, w_mv=w_mv, w_mo=w_mo, q_norm_m=q_norm_m, k_norm_m=k_norm_m,
             norm_ffn_g=norm_ffn_g, w_gate=w_gate, w_up=w_up, w_down=w_down)
    h = x
    for l in range(w_in.shape[0]):
        h = _layer(h, mem, l, p)
    return h
```

```python
import functools
import math

import jax
import jax.numpy as jnp
from jax import lax
from jax.experimental import pallas as pl
from jax.experimental.pallas import tpu as pltpu

F32 = jnp.float32
BF16 = jnp.bfloat16

D_MODEL = 4096
HEAD_DIM = 128
MOBA_HEADS = 16
DIFF_HEADS = 8
MOBA_WIDTH = MOBA_HEADS * HEAD_DIM
DIFF_V_DIM = 2 * HEAD_DIM
IN_WIDTH = 12288
MOBA_BLOCK = 256
MOBA_TOPK = 3
ROPE_THETA = 500000.0
ROT_DIM = HEAD_DIM // 4
MEM_HEADS = 4
MEM_WIDTH = MEM_HEADS * HEAD_DIM
NORM_EPS = 1e-6
ATTN_SCALE = HEAD_DIM ** -0.5
MASKED = -1e30
EXP2_SCALE = ATTN_SCALE * math.log2(math.e)

V7X_VMEM_BYTES = 64 * 1024 * 1024
VMEM_LIMIT = V7X_VMEM_BYTES - 8 * 1024 * 1024

NT_DIMS = (((1,), (1,)), ((), ()))


def _params(n_grid):
    return pltpu.CompilerParams(
        dimension_semantics=("parallel",) * (n_grid - 1) + ("arbitrary",),
        vmem_limit_bytes=VMEM_LIMIT,
    )


def _rmsnorm_kernel(x_ref, g_ref, o_ref):
    x = x_ref[...].astype(F32)
    ms = jnp.mean(x * x, axis=-1, keepdims=True)
    o_ref[...] = (x * lax.rsqrt(ms + NORM_EPS) * g_ref[...]).astype(o_ref.dtype)


def _rmsnorm(x, g, tm):
    m, d = x.shape
    return pl.pallas_call(
        _rmsnorm_kernel,
        grid=(m // tm,),
        in_specs=[pl.BlockSpec((tm, d), lambda i: (i, 0)), pl.BlockSpec((1, d), lambda i: (0, 0))],
        out_specs=pl.BlockSpec((tm, d), lambda i: (i, 0)),
        out_shape=jax.ShapeDtypeStruct((m, d), BF16),
        compiler_params=_params(1),
        name="rmsnorm",
    )(x, g.reshape(1, d).astype(F32))


def _mm_res_kernel(x_ref, w_ref, r_ref, o_ref):
    o_ref[...] = r_ref[...] + jnp.dot(x_ref[...], w_ref[...], preferred_element_type=F32)


def _matmul_residual(x, w, r, tm, tn):
    m, k = x.shape
    n = w.shape[1]
    return pl.pallas_call(
        _mm_res_kernel,
        grid=(m // tm, n // tn),
        in_specs=[
            pl.BlockSpec((tm, k), lambda i, j: (i, 0)),
            pl.BlockSpec((k, tn), lambda i, j: (0, j)),
            pl.BlockSpec((tm, tn), lambda i, j: (i, j)),
        ],
        out_specs=pl.BlockSpec((tm, tn), lambda i, j: (i, j)),
        out_shape=jax.ShapeDtypeStruct((m, n), F32),
        compiler_params=_params(2),
        name="matmul_residual",
    )(x, w, r)


def _mm2_res_kernel(a_ref, b_ref, wa_ref, wb_ref, r_ref, o_ref):
    acc = jnp.dot(a_ref[...], wa_ref[...].astype(BF16), preferred_element_type=F32)
    acc = acc + jnp.dot(b_ref[...], wb_ref[...].astype(BF16), preferred_element_type=F32)
    o_ref[...] = r_ref[...] + acc


def _out_projection(a, b, w, r, tm, tn):
    m, ka = a.shape
    kb = b.shape[1]
    assert ka == kb and w.shape[0] == ka + kb
    n = w.shape[1]
    return pl.pallas_call(
        _mm2_res_kernel,
        grid=(m // tm, n // tn),
        in_specs=[
            pl.BlockSpec((tm, ka), lambda i, j: (i, 0)),
            pl.BlockSpec((tm, kb), lambda i, j: (i, 0)),
            pl.BlockSpec((ka, tn), lambda i, j: (0, j)),
            pl.BlockSpec((kb, tn), lambda i, j: (1, j)),
            pl.BlockSpec((tm, tn), lambda i, j: (i, j)),
        ],
        out_specs=pl.BlockSpec((tm, tn), lambda i, j: (i, j)),
        out_shape=jax.ShapeDtypeStruct((m, n), F32),
        compiler_params=_params(2),
        name="out_projection",
    )(a, b, w, w, r)


def _swiglu_kernel(x_ref, wg_ref, wu_ref, o_ref):
    x = x_ref[...]
    g = jnp.dot(x, wg_ref[...].astype(BF16), preferred_element_type=F32)
    u = jnp.dot(x, wu_ref[...].astype(BF16), preferred_element_type=F32)
    o_ref[...] = (jax.nn.silu(g) * u).astype(o_ref.dtype)


def _swiglu_up(x, wg, wu, tm, tn):
    m, k = x.shape
    n = wg.shape[1]
    return pl.pallas_call(
        _swiglu_kernel,
        grid=(m // tm, n // tn),
        in_specs=[
            pl.BlockSpec((tm, k), lambda i, j: (i, 0)),
            pl.BlockSpec((k, tn), lambda i, j: (0, j)),
            pl.BlockSpec((k, tn), lambda i, j: (0, j)),
        ],
        out_specs=pl.BlockSpec((tm, tn), lambda i, j: (i, j)),
        out_shape=jax.ShapeDtypeStruct((m, n), BF16),
        compiler_params=_params(2),
        name="swiglu_up",
    )(x, wg, wu)


def _rope_tables(seq):
    half = ROT_DIM // 2
    pos = jnp.arange(seq, dtype=F32)
    inv = ROPE_THETA ** (-jnp.arange(0, ROT_DIM, 2, dtype=F32) / ROT_DIM)
    ang = pos[:, None] * inv[None, :]
    cos, sin = jnp.cos(ang), jnp.sin(ang)
    rest = HEAD_DIM - ROT_DIM
    c = jnp.concatenate([cos, cos, jnp.ones((seq, rest), F32)], axis=-1)
    sa = jnp.concatenate([-sin, jnp.zeros((seq, HEAD_DIM - half), F32)], axis=-1)
    sb = jnp.concatenate([jnp.zeros((seq, half), F32), sin, jnp.zeros((seq, rest), F32)], axis=-1)
    return c, sa, sb


def _head_gains(p, l):
    ones = jnp.ones((HEAD_DIM,), F32)
    rows = [p["q_norm_a"][l].astype(F32) * EXP2_SCALE, p["k_norm_a"][l].astype(F32), ones,
            p["q_norm_b"][l].astype(F32) * EXP2_SCALE, p["k_norm_b"][l].astype(F32), ones]
    return jnp.stack(rows).reshape(6, 1, HEAD_DIM)


def _norm_rope(x, g, c, sa, sb):
    half = ROT_DIM // 2
    ms = jnp.mean(x * x, axis=-1, keepdims=True)
    y = x * lax.rsqrt(ms + NORM_EPS) * g
    up = pltpu.roll(y, HEAD_DIM - half, 1)
    down = pltpu.roll(y, half, 1)
    return y * c + up * sa + down * sb


def _in_proj_kernel(nj, x_ref, w_ref, g_ref, c_ref, sa_ref, sb_ref, o_ref, acc_ref):
    t = pl.program_id(0)
    tn = o_ref.shape[1]

    @pl.when(t == 0)
    def _():
        acc_ref[...] = jnp.zeros_like(acc_ref)

    def next_tile():
        acc_ref[...] = jnp.dot(x_ref[...], w_ref[...].astype(BF16), preferred_element_type=F32)

    region = ((jnp.maximum(t - 1, 0) % nj) * tn) // MOBA_WIDTH
    is_value = region % 3 == 2

    @pl.when(is_value)
    def _():
        prev = acc_ref[...]
        next_tile()
        o_ref[...] = prev.astype(o_ref.dtype)

    @pl.when(jnp.logical_not(is_value))
    def _():
        prev = acc_ref[...]
        next_tile()
        for h in range(tn // HEAD_DIM):
            cols = slice(h * HEAD_DIM, (h + 1) * HEAD_DIM)
            out = _norm_rope(prev[:, cols], g_ref[0], c_ref[...], sa_ref[...], sb_ref[...])
            o_ref[:, cols] = out.astype(o_ref.dtype)


def _in_projection(xn, w, gains, tables, seq, tm, tn):
    m, k = xn.shape
    n = w.shape[1]
    assert MOBA_WIDTH % tn == 0 and seq % tm == 0
    nj = n // tn
    tiles = (m // tm) * nj
    mm = lambda t: jnp.minimum(t, tiles - 1)
    ep = lambda t: jnp.maximum(t - 1, 0)
    region = lambda t: ((ep(t) % nj) * tn) // MOBA_WIDTH
    table_spec = pl.BlockSpec((tm, HEAD_DIM), lambda t: ((ep(t) // nj) % (seq // tm), 0))
    return pl.pallas_call(
        functools.partial(_in_proj_kernel, nj),
        grid=(tiles + 1,),
        in_specs=[pl.BlockSpec((tm, k), lambda t: (mm(t) // nj, 0)),
                  pl.BlockSpec((k, tn), lambda t: (0, mm(t) % nj)),
                  pl.BlockSpec((1, 1, HEAD_DIM), lambda t: (region(t), 0, 0)),
                  table_spec, table_spec, table_spec],
        out_specs=pl.BlockSpec((tm, tn), lambda t: (ep(t) // nj, ep(t) % nj)),
        out_shape=jax.ShapeDtypeStruct((m, n), BF16),
        scratch_shapes=[pltpu.VMEM((tm, tn), F32)],
        compiler_params=_params(1),
        name="in_projection",
    )(xn, w, gains, *tables)


def _causal_mask(s):
    key = lax.broadcasted_iota(jnp.int32, s.shape, 0)
    qry = lax.broadcasted_iota(jnp.int32, s.shape, 1)
    return jnp.where(key <= qry, s, MASKED)


def _prefix_softmax(blocks):
    m = jnp.max(blocks[0], axis=0, keepdims=True)
    for sj in blocks[1:]:
        m = jnp.maximum(m, jnp.max(sj, axis=0, keepdims=True))
    ps, l = [], None
    for sj in blocks:
        pj = jnp.exp2(sj - m)
        lj = jnp.sum(pj, axis=0, keepdims=True)
        l = lj if l is None else l + lj
        ps.append(pj)
    return ps, l


def _split_blocks(s, n):
    return [s[j * MOBA_BLOCK:(j + 1) * MOBA_BLOCK, :] for j in range(n)]


def _moba_kernel(q_ref, k_ref, v_ref, o_ref, vt_ref, kmean_ref):
    blk = MOBA_BLOCK
    nb = q_ref.shape[1] // blk

    for n in range(nb):
        rows = slice(n * blk, (n + 1) * blk)
        kmean_ref[n:n + 1, :] = jnp.mean(k_ref[0, rows, :].astype(F32), axis=0, keepdims=True)
        vt_ref[:, rows] = v_ref[0, rows, :].astype(F32).T.astype(BF16)

    def masked_scores(qi):
        rows = slice(qi * blk, (qi + 1) * blk)
        nk = (qi + 1) * blk
        qb = q_ref[0, rows, :]
        s = lax.dot_general(k_ref[0, 0:nk, :], qb, NT_DIMS, preferred_element_type=F32)
        blocks = _split_blocks(s, qi + 1)
        blocks[qi] = _causal_mask(blocks[qi])

        if qi > MOBA_TOPK:
            gate = lax.dot_general(kmean_ref[...], qb.astype(F32), NT_DIMS, preferred_element_type=F32,
                                   precision=lax.Precision.HIGHEST)
            n_iota = lax.broadcasted_iota(jnp.int32, gate.shape, 0)
            beaten = jnp.zeros(gate.shape, jnp.int32)
            for mth in range(qi):
                gm = gate[mth:mth + 1, :]
                wins = (gm > gate) | ((gm == gate) & (mth < n_iota))
                beaten = beaten + wins.astype(jnp.int32)
            bias = jnp.where(beaten < MOBA_TOPK, 0.0, MASKED)
            for j in range(qi):
                blocks[j] = blocks[j] + bias[j:j + 1, :]
        return blocks

    def softmax(blocks):
        ps, l = _prefix_softmax(blocks)
        return jnp.concatenate([pj.astype(BF16) for pj in ps], axis=0), l

    def weighted_values(qi, p, l):
        rows = slice(qi * blk, (qi + 1) * blk)
        acc = jnp.dot(vt_ref[:, 0:(qi + 1) * blk], p, preferred_element_type=F32)
        o_ref[0, rows, :] = (acc / l).T.astype(o_ref.dtype)

    ahead = masked_scores(0)
    for qi in range(nb):
        blocks = ahead
        if qi + 1 < nb:
            ahead = masked_scores(qi + 1)
        weighted_values(qi, *softmax(blocks))


def _moba_attention(proj):
    b, s, _ = proj.shape
    nb = s // MOBA_BLOCK
    head = lambda off: pl.BlockSpec((1, s, HEAD_DIM), lambda bi, h: (bi, 0, off + h))
    return pl.pallas_call(
        _moba_kernel,
        grid=(b, MOBA_HEADS),
        in_specs=[head(0), head(MOBA_HEADS), head(2 * MOBA_HEADS)],
        out_specs=pl.BlockSpec((1, s, HEAD_DIM), lambda bi, h: (bi, 0, h)),
        out_shape=jax.ShapeDtypeStruct((b, s, MOBA_WIDTH), BF16),
        scratch_shapes=[
            pltpu.VMEM((HEAD_DIM, s), BF16),
            pltpu.VMEM((nb, HEAD_DIM), F32),
        ],
        compiler_params=_params(2),
        name="moba_attention",
    )(proj, proj, proj)


def _diff_kernel(lam_init, q_ref, k_ref, v_ref, lq1_ref, lk1_ref, lq2_ref, lk2_ref, gs_ref,
                 o_ref, vt_ref):
    blk = MOBA_BLOCK
    nb = q_ref.shape[1] // blk
    hd = HEAD_DIM

    lam = (jnp.exp(jnp.sum(lq1_ref[...] * lk1_ref[...], axis=-1, keepdims=True))
           - jnp.exp(jnp.sum(lq2_ref[...] * lk2_ref[...], axis=-1, keepdims=True))
           + lam_init)

    for n in range(nb):
        rows = slice(n * blk, (n + 1) * blk)
        vt_ref[:, rows] = v_ref[0, rows, :].astype(F32).T.astype(BF16)

    def masked_scores(qi):
        rows = slice(qi * blk, (qi + 1) * blk)
        nk = (qi + 1) * blk
        out = []
        for half in range(2):
            cols = slice(half * hd, (half + 1) * hd)
            s = lax.dot_general(k_ref[0, 0:nk, cols], q_ref[0, rows, cols], NT_DIMS,
                                preferred_element_type=F32)
            blocks = _split_blocks(s, qi + 1)
            blocks[qi] = _causal_mask(blocks[qi])
            out.append(blocks)
        return out

    def attend(qi, blocks1, blocks2):
        rows = slice(qi * blk, (qi + 1) * blk)
        nk = (qi + 1) * blk
        p1, l1 = _prefix_softmax(blocks1)
        p2, l2 = _prefix_softmax(blocks2)
        r1 = 1.0 / l1
        r2 = lam / l2
        w = jnp.concatenate([(a * r1 - b * r2).astype(BF16) for a, b in zip(p1, p2)], axis=0)
        o = jnp.dot(vt_ref[:, 0:nk], w, preferred_element_type=F32)
        ms = jnp.mean(o * o, axis=0, keepdims=True)
        on = (o * lax.rsqrt(ms + NORM_EPS)).T * gs_ref[...]
        o_ref[0, rows, :] = (on * (1.0 - lam_init)).astype(o_ref.dtype)

    ahead = masked_scores(0)
    for qi in range(nb):
        blocks = ahead
        if qi + 1 < nb:
            ahead = masked_scores(qi + 1)
        attend(qi, *blocks)


def _diff_attention(proj, lq1, lk1, lq2, lk2, gs, lam_init):
    b, s, _ = proj.shape
    qoff = 3 * MOBA_WIDTH // DIFF_V_DIM
    head = lambda off: pl.BlockSpec((1, s, DIFF_V_DIM), lambda bi, h: (bi, 0, off + h))
    full = lambda shape: pl.BlockSpec(shape, lambda bi, h: (0,) * len(shape))
    row = lambda v: v.reshape(1, -1).astype(F32)
    return pl.pallas_call(
        functools.partial(_diff_kernel, lam_init),
        grid=(b, DIFF_HEADS),
        in_specs=[head(qoff), head(qoff + DIFF_HEADS), head(qoff + 2 * DIFF_HEADS),
                  full((1, HEAD_DIM)), full((1, HEAD_DIM)), full((1, HEAD_DIM)), full((1, HEAD_DIM)),
                  full((1, DIFF_V_DIM))],
        out_specs=pl.BlockSpec((1, s, DIFF_V_DIM), lambda bi, h: (bi, 0, h)),
        out_shape=jax.ShapeDtypeStruct((b, s, DIFF_HEADS * DIFF_V_DIM), BF16),
        scratch_shapes=[pltpu.VMEM((DIFF_V_DIM, s), BF16)],
        compiler_params=_params(2),
        name="diff_attention",
    )(proj, proj, proj, row(lq1), row(lk1), row(lq2), row(lk2), row(gs))


def _mem_kv_kernel(m_ref, wk_ref, wv_ref, gk_ref, k_ref, v_ref):
    x = m_ref[...]
    k = jnp.dot(x, wk_ref[...].astype(BF16), preferred_element_type=F32)
    for h in range(MEM_HEADS):
        cols = slice(h * HEAD_DIM, (h + 1) * HEAD_DIM)
        kh = k[:, cols]
        ms = jnp.mean(kh * kh, axis=-1, keepdims=True)
        k_ref[:, cols] = (kh * lax.rsqrt(ms + NORM_EPS) * gk_ref[...]).astype(k_ref.dtype)
    v_ref[...] = jnp.dot(x, wv_ref[...].astype(BF16), preferred_element_type=F32).astype(v_ref.dtype)


def _mem_kv(mem_n, wk, wv, gk, tm):
    m, d = mem_n.shape
    return pl.pallas_call(
        _mem_kv_kernel,
        grid=(m // tm,),
        in_specs=[pl.BlockSpec((tm, d), lambda i: (i, 0)),
                  pl.BlockSpec((d, MEM_WIDTH), lambda i: (0, 0)),
                  pl.BlockSpec((d, MEM_WIDTH), lambda i: (0, 0)),
                  pl.BlockSpec((1, HEAD_DIM), lambda i: (0, 0))],
        out_specs=[pl.BlockSpec((tm, MEM_WIDTH), lambda i: (i, 0)),
                   pl.BlockSpec((tm, MEM_WIDTH), lambda i: (i, 0))],
        out_shape=[jax.ShapeDtypeStruct((m, MEM_WIDTH), BF16), jax.ShapeDtypeStruct((m, MEM_WIDTH), BF16)],
        compiler_params=_params(1),
        name="mem_kv",
    )(mem_n, wk, wv, gk.reshape(1, -1).astype(F32))


def _row_rmsnorm(x, g):
    ms = jnp.mean(x * x, axis=-1, keepdims=True)
    return x * lax.rsqrt(ms + NORM_EPS) * g


def _cross_kernel(h_ref, k_ref, v_ref, wq_ref, wo_ref, gq_ref, gin_ref, gout_ref, o_ref, on_ref):
    hn = _row_rmsnorm(h_ref[...], gin_ref[...]).astype(BF16)
    q = jnp.dot(hn, wq_ref[...], preferred_element_type=F32)
    heads = []
    for h in range(MEM_HEADS):
        cols = slice(h * HEAD_DIM, (h + 1) * HEAD_DIM)
        qh = q[:, cols]
        ms = jnp.mean(qh * qh, axis=-1, keepdims=True)
        qh = (qh * lax.rsqrt(ms + NORM_EPS) * gq_ref[...]).astype(BF16)
        s = lax.dot_general(qh, k_ref[0, :, cols], NT_DIMS, preferred_element_type=F32) * ATTN_SCALE
        e = jnp.exp(s - jnp.max(s, axis=-1, keepdims=True))
        p = (e / jnp.sum(e, axis=-1, keepdims=True)).astype(BF16)
        heads.append(jnp.dot(p, v_ref[0, :, cols], preferred_element_type=F32).astype(BF16))
    o = jnp.concatenate(heads, axis=-1)
    out = h_ref[...] + jnp.dot(o, wo_ref[...], preferred_element_type=F32)
    o_ref[...] = out
    on_ref[...] = _row_rmsnorm(out, gout_ref[...]).astype(on_ref.dtype)


def _cross_attention(h, k, v, wq, wo, gq, g_in, g_out, seq, tm):
    m, d = h.shape
    mem_len = k.shape[1]
    per_batch = seq // tm
    row = lambda g: g.reshape(1, -1).astype(F32)
    return pl.pallas_call(
        _cross_kernel,
        grid=(m // tm,),
        in_specs=[pl.BlockSpec((tm, d), lambda i: (i, 0)),
                  pl.BlockSpec((1, mem_len, MEM_WIDTH), lambda i: (i // per_batch, 0, 0)),
                  pl.BlockSpec((1, mem_len, MEM_WIDTH), lambda i: (i // per_batch, 0, 0)),
                  pl.BlockSpec((d, MEM_WIDTH), lambda i: (0, 0)),
                  pl.BlockSpec((MEM_WIDTH, d), lambda i: (0, 0)),
                  pl.BlockSpec((1, HEAD_DIM), lambda i: (0, 0)),
                  pl.BlockSpec((1, d), lambda i: (0, 0)),
                  pl.BlockSpec((1, d), lambda i: (0, 0))],
        out_specs=[pl.BlockSpec((tm, d), lambda i: (i, 0)), pl.BlockSpec((tm, d), lambda i: (i, 0))],
        out_shape=[jax.ShapeDtypeStruct((m, d), F32), jax.ShapeDtypeStruct((m, d), BF16)],
        compiler_params=_params(1),
        name="cross_attention",
    )(h, k, v, wq, wo, row(gq), row(g_in), row(g_out))


def _layer(h, mem_n_src, l, p):
    b, s, d = h.shape
    m = b * s
    lam_init = 0.8 - 0.6 * math.exp(-0.3 * l)
    tables = _rope_tables(s)
    bf = lambda w: w.astype(BF16)
    h2d = h.reshape(m, d)

    xn = _rmsnorm(h2d, p["norm_mix_g"][l], 256)
    proj = _in_projection(xn, p["w_in"][l], _head_gains(p, l), tables, s, 1024, 512).reshape(b, s, IN_WIDTH)
    out_a = _moba_attention(proj)
    out_b = _diff_attention(proj, p["lam_q1"][l], p["lam_k1"][l], p["lam_q2"][l], p["lam_k2"][l],
                            p["diff_subln_g"][l], lam_init)
    h2d = _out_projection(out_a.reshape(m, -1), out_b.reshape(m, -1), p["w_out"][l], h2d, 1024, 512)

    mem_len = mem_n_src.shape[1]
    mem_n = _rmsnorm(mem_n_src.reshape(b * mem_len, d), p["norm_mem_g"][l], 256)
    mk, mv = _mem_kv(mem_n, p["w_mk"][l], p["w_mv"][l], p["k_norm_m"][l], 256)
    h2d, hn = _cross_attention(h2d, mk.reshape(b, mem_len, -1), mv.reshape(b, mem_len, -1),
                               bf(p["w_mq"][l]), bf(p["w_mo"][l]), p["q_norm_m"][l],
                               p["norm_cross_g"][l], p["norm_ffn_g"][l], s, 256)

    act = _swiglu_up(hn, p["w_gate"][l], p["w_up"][l], 1024, 256)
    h2d = _matmul_residual(act, bf(p["w_down"][l]), h2d, 512, 512)
    return h2d.reshape(b, s, d)


def kernel(x, mem, norm_mix_g, w_in, q_norm_a, k_norm_a, q_norm_b, k_norm_b, lam_q1, lam_k1, lam_q2, lam_k2, diff_subln_g, w_out, norm_cross_g, norm_mem_g, w_mq, w_mk, w_mv, w_mo, q_norm_m, k_norm_m, norm_ffn_g, w_gate, w_up, w_down):
    p = dict(norm_mix_g=norm_mix_g, w_in=w_in, q_norm_a=q_norm_a, k_norm_a=k_norm_a, q_norm_b=q_norm_b,
             k_norm_b=k_norm_b, lam_q1=lam_q1, lam_k1=lam_k1, lam_q2=lam_q2, lam_k2=lam_k2,
             diff_subln_g=diff_subln_g, w_out=w_out, norm_cross_g=norm_cross_g, norm_mem_g=norm_mem_g,
             w_mq=w_mq, w_mk=w_mk, w_mv=w_mv, w_mo=w_mo, q_norm_m=q_norm_m, k_norm_m=k_norm_m,
             norm_ffn_g=norm_ffn_g, w_gate=w_gate, w_up=w_up, w_down=w_down)
    h = x
    for l in range(w_in.shape[0]):
        h = _layer(h, mem, l, p)
    return h
```

```python
import functools
import math

import jax
import jax.numpy as jnp
from jax import lax
from jax.experimental import pallas as pl
from jax.experimental.pallas import tpu as pltpu

F32 = jnp.float32
BF16 = jnp.bfloat16

D_MODEL = 4096
HEAD_DIM = 128
MOBA_HEADS = 16
DIFF_HEADS = 8
MOBA_WIDTH = MOBA_HEADS * HEAD_DIM
DIFF_V_DIM = 2 * HEAD_DIM
IN_WIDTH = 12288
MOBA_BLOCK = 256
MOBA_TOPK = 3
MOBA_HEADS_PER_STEP = 2
ROPE_THETA = 500000.0
ROT_DIM = HEAD_DIM // 4
MEM_HEADS = 4
MEM_WIDTH = MEM_HEADS * HEAD_DIM
NORM_EPS = 1e-6
ATTN_SCALE = HEAD_DIM ** -0.5
MASKED = -1e30
EXP2_SCALE = ATTN_SCALE * math.log2(math.e)

V7X_VMEM_BYTES = 64 * 1024 * 1024
VMEM_LIMIT = V7X_VMEM_BYTES - 8 * 1024 * 1024

NT_DIMS = (((1,), (1,)), ((), ()))


def _params(n_grid):
    return pltpu.CompilerParams(
        dimension_semantics=("parallel",) * (n_grid - 1) + ("arbitrary",),
        vmem_limit_bytes=VMEM_LIMIT,
    )


def _rmsnorm_kernel(x_ref, g_ref, o_ref):
    x = x_ref[...].astype(F32)
    ms = jnp.mean(x * x, axis=-1, keepdims=True)
    o_ref[...] = (x * lax.rsqrt(ms + NORM_EPS) * g_ref[...]).astype(o_ref.dtype)


def _rmsnorm(x, g, tm):
    m, d = x.shape
    return pl.pallas_call(
        _rmsnorm_kernel,
        grid=(m // tm,),
        in_specs=[pl.BlockSpec((tm, d), lambda i: (i, 0)), pl.BlockSpec((1, d), lambda i: (0, 0))],
        out_specs=pl.BlockSpec((tm, d), lambda i: (i, 0)),
        out_shape=jax.ShapeDtypeStruct((m, d), BF16),
        compiler_params=_params(1),
        name="rmsnorm",
    )(x, g.reshape(1, d).astype(F32))


def _mm_res_kernel(x_ref, w_ref, r_ref, o_ref):
    o_ref[...] = r_ref[...] + jnp.dot(x_ref[...], w_ref[...], preferred_element_type=F32)


def _matmul_residual(x, w, r, tm, tn):
    m, k = x.shape
    n = w.shape[1]
    return pl.pallas_call(
        _mm_res_kernel,
        grid=(m // tm, n // tn),
        in_specs=[
            pl.BlockSpec((tm, k), lambda i, j: (i, 0)),
            pl.BlockSpec((k, tn), lambda i, j: (0, j)),
            pl.BlockSpec((tm, tn), lambda i, j: (i, j)),
        ],
        out_specs=pl.BlockSpec((tm, tn), lambda i, j: (i, j)),
        out_shape=jax.ShapeDtypeStruct((m, n), F32),
        compiler_params=_params(2),
        name="matmul_residual",
    )(x, w, r)


def _mm2_res_kernel(a_ref, b_ref, wa_ref, wb_ref, r_ref, o_ref):
    acc = jnp.dot(a_ref[...], wa_ref[...].astype(BF16), preferred_element_type=F32)
    acc = acc + jnp.dot(b_ref[...], wb_ref[...].astype(BF16), preferred_element_type=F32)
    o_ref[...] = r_ref[...] + acc


def _out_projection(a, b, w, r, tm, tn):
    m, ka = a.shape
    kb = b.shape[1]
    assert ka == kb and w.shape[0] == ka + kb
    n = w.shape[1]
    return pl.pallas_call(
        _mm2_res_kernel,
        grid=(m // tm, n // tn),
        in_specs=[
            pl.BlockSpec((tm, ka), lambda i, j: (i, 0)),
            pl.BlockSpec((tm, kb), lambda i, j: (i, 0)),
            pl.BlockSpec((ka, tn), lambda i, j: (0, j)),
            pl.BlockSpec((kb, tn), lambda i, j: (1, j)),
            pl.BlockSpec((tm, tn), lambda i, j: (i, j)),
        ],
        out_specs=pl.BlockSpec((tm, tn), lambda i, j: (i, j)),
        out_shape=jax.ShapeDtypeStruct((m, n), F32),
        compiler_params=_params(2),
        name="out_projection",
    )(a, b, w, w, r)


def _swiglu_kernel(x_ref, wg_ref, wu_ref, o_ref):
    x = x_ref[...]
    g = jnp.dot(x, wg_ref[...].astype(BF16), preferred_element_type=F32)
    u = jnp.dot(x, wu_ref[...].astype(BF16), preferred_element_type=F32)
    o_ref[...] = (jax.nn.silu(g) * u).astype(o_ref.dtype)


def _swiglu_up(x, wg, wu, tm, tn):
    m, k = x.shape
    n = wg.shape[1]
    return pl.pallas_call(
        _swiglu_kernel,
        grid=(m // tm, n // tn),
        in_specs=[
            pl.BlockSpec((tm, k), lambda i, j: (i, 0)),
            pl.BlockSpec((k, tn), lambda i, j: (0, j)),
            pl.BlockSpec((k, tn), lambda i, j: (0, j)),
        ],
        out_specs=pl.BlockSpec((tm, tn), lambda i, j: (i, j)),
        out_shape=jax.ShapeDtypeStruct((m, n), BF16),
        compiler_params=_params(2),
        name="swiglu_up",
    )(x, wg, wu)


def _rope_tables(seq):
    half = ROT_DIM // 2
    pos = jnp.arange(seq, dtype=F32)
    inv = ROPE_THETA ** (-jnp.arange(0, ROT_DIM, 2, dtype=F32) / ROT_DIM)
    ang = pos[:, None] * inv[None, :]
    cos, sin = jnp.cos(ang), jnp.sin(ang)
    rest = HEAD_DIM - ROT_DIM
    c = jnp.concatenate([cos, cos, jnp.ones((seq, rest), F32)], axis=-1)
    sa = jnp.concatenate([-sin, jnp.zeros((seq, HEAD_DIM - half), F32)], axis=-1)
    sb = jnp.concatenate([jnp.zeros((seq, half), F32), sin, jnp.zeros((seq, rest), F32)], axis=-1)
    return c, sa, sb


def _head_gains(p, l):
    ones = jnp.ones((HEAD_DIM,), F32)
    rows = [p["q_norm_a"][l].astype(F32) * EXP2_SCALE, p["k_norm_a"][l].astype(F32), ones,
            p["q_norm_b"][l].astype(F32) * EXP2_SCALE, p["k_norm_b"][l].astype(F32), ones]
    return jnp.stack(rows).reshape(6, 1, HEAD_DIM)


def _norm_rope(x, g, c, sa, sb):
    half = ROT_DIM // 2
    ms = jnp.mean(x * x, axis=-1, keepdims=True)
    y = x * lax.rsqrt(ms + NORM_EPS) * g
    up = pltpu.roll(y, HEAD_DIM - half, 1)
    down = pltpu.roll(y, half, 1)
    return y * c + up * sa + down * sb


def _in_proj_kernel(nj, x_ref, w_ref, g_ref, c_ref, sa_ref, sb_ref, o_ref, acc_ref):
    t = pl.program_id(0)
    tn = o_ref.shape[1]

    @pl.when(t == 0)
    def _():
        acc_ref[...] = jnp.zeros_like(acc_ref)

    def next_tile():
        acc_ref[...] = jnp.dot(x_ref[...], w_ref[...].astype(BF16), preferred_element_type=F32)

    region = ((jnp.maximum(t - 1, 0) % nj) * tn) // MOBA_WIDTH
    is_value = region % 3 == 2

    @pl.when(is_value)
    def _():
        prev = acc_ref[...]
        next_tile()
        o_ref[...] = prev.astype(o_ref.dtype)

    @pl.when(jnp.logical_not(is_value))
    def _():
        prev = acc_ref[...]
        next_tile()
        for h in range(tn // HEAD_DIM):
            cols = slice(h * HEAD_DIM, (h + 1) * HEAD_DIM)
            out = _norm_rope(prev[:, cols], g_ref[0], c_ref[...], sa_ref[...], sb_ref[...])
            o_ref[:, cols] = out.astype(o_ref.dtype)


def _in_projection(xn, w, gains, tables, seq, tm, tn):
    m, k = xn.shape
    n = w.shape[1]
    assert MOBA_WIDTH % tn == 0 and seq % tm == 0
    nj = n // tn
    tiles = (m // tm) * nj
    mm = lambda t: jnp.minimum(t, tiles - 1)
    ep = lambda t: jnp.maximum(t - 1, 0)
    region = lambda t: ((ep(t) % nj) * tn) // MOBA_WIDTH
    table_spec = pl.BlockSpec((tm, HEAD_DIM), lambda t: ((ep(t) // nj) % (seq // tm), 0))
    return pl.pallas_call(
        functools.partial(_in_proj_kernel, nj),
        grid=(tiles + 1,),
        in_specs=[pl.BlockSpec((tm, k), lambda t: (mm(t) // nj, 0)),
                  pl.BlockSpec((k, tn), lambda t: (0, mm(t) % nj)),
                  pl.BlockSpec((1, 1, HEAD_DIM), lambda t: (region(t), 0, 0)),
                  table_spec, table_spec, table_spec],
        out_specs=pl.BlockSpec((tm, tn), lambda t: (ep(t) // nj, ep(t) % nj)),
        out_shape=jax.ShapeDtypeStruct((m, n), BF16),
        scratch_shapes=[pltpu.VMEM((tm, tn), F32)],
        compiler_params=_params(1),
        name="in_projection",
    )(xn, w, gains, *tables)


def _causal_mask(s):
    key = lax.broadcasted_iota(jnp.int32, s.shape, 0)
    qry = lax.broadcasted_iota(jnp.int32, s.shape, 1)
    return jnp.where(key <= qry, s, MASKED)


def _prefix_softmax(blocks):
    m = jnp.max(blocks[0], axis=0, keepdims=True)
    for sj in blocks[1:]:
        m = jnp.maximum(m, jnp.max(sj, axis=0, keepdims=True))
    ps, l = [], None
    for sj in blocks:
        pj = jnp.exp2(sj - m)
        lj = jnp.sum(pj, axis=0, keepdims=True)
        l = lj if l is None else l + lj
        ps.append(pj)
    return ps, l


def _split_blocks(s, n):
    return [s[j * MOBA_BLOCK:(j + 1) * MOBA_BLOCK, :] for j in range(n)]


def _moba_kernel(q_ref, k_ref, v_ref, o_ref, vt_ref, kmean_ref):
    blk = MOBA_BLOCK
    nb = q_ref.shape[1] // blk
    heads = q_ref.shape[2] // HEAD_DIM
    lanes = lambda h: slice(h * HEAD_DIM, (h + 1) * HEAD_DIM)

    for h in range(heads):
        for n in range(nb):
            rows = slice(n * blk, (n + 1) * blk)
            kmean_ref[h, n:n + 1, :] = jnp.mean(k_ref[0, rows, lanes(h)].astype(F32), axis=0, keepdims=True)
            vt_ref[h, :, rows] = v_ref[0, rows, lanes(h)].astype(F32).T.astype(BF16)

    def masked_scores(h, qi):
        rows = slice(qi * blk, (qi + 1) * blk)
        nk = (qi + 1) * blk
        qb = q_ref[0, rows, lanes(h)]
        s = lax.dot_general(k_ref[0, 0:nk, lanes(h)], qb, NT_DIMS, preferred_element_type=F32)
        blocks = _split_blocks(s, qi + 1)
        blocks[qi] = _causal_mask(blocks[qi])

        if qi > MOBA_TOPK:
            gate = lax.dot_general(kmean_ref[h], qb.astype(F32), NT_DIMS, preferred_element_type=F32,
                                   precision=lax.Precision.HIGHEST)
            n_iota = lax.broadcasted_iota(jnp.int32, gate.shape, 0)
            beaten = jnp.zeros(gate.shape, jnp.int32)
            for mth in range(qi):
                gm = gate[mth:mth + 1, :]
                wins = (gm > gate) | ((gm == gate) & (mth < n_iota))
                beaten = beaten + wins.astype(jnp.int32)
            bias = jnp.where(beaten < MOBA_TOPK, 0.0, MASKED)
            for j in range(qi):
                blocks[j] = blocks[j] + bias[j:j + 1, :]
        return blocks

    def softmax(blocks):
        ps, l = _prefix_softmax(blocks)
        return jnp.concatenate([pj.astype(BF16) for pj in ps], axis=0), l

    def weighted_values(h, qi, p, l):
        rows = slice(qi * blk, (qi + 1) * blk)
        acc = jnp.dot(vt_ref[h, :, 0:(qi + 1) * blk], p, preferred_element_type=F32)
        o_ref[0, rows, lanes(h)] = (acc / l).T.astype(o_ref.dtype)

    items = [(h, qi) for qi in range(nb) for h in range(heads)]
    ahead = masked_scores(*items[0])
    for idx, item in enumerate(items):
        blocks = ahead
        if idx + 1 < len(items):
            ahead = masked_scores(*items[idx + 1])
        weighted_values(*item, *softmax(blocks))


def _moba_attention(proj):
    b, s, _ = proj.shape
    nb = s // MOBA_BLOCK
    hp = MOBA_HEADS_PER_STEP
    width = hp * HEAD_DIM
    head = lambda off: pl.BlockSpec((1, s, width), lambda bi, h: (bi, 0, off + h))
    return pl.pallas_call(
        _moba_kernel,
        grid=(b, MOBA_HEADS // hp),
        in_specs=[head(0), head(MOBA_HEADS // hp), head(2 * MOBA_HEADS // hp)],
        out_specs=pl.BlockSpec((1, s, width), lambda bi, h: (bi, 0, h)),
        out_shape=jax.ShapeDtypeStruct((b, s, MOBA_WIDTH), BF16),
        scratch_shapes=[
            pltpu.VMEM((hp, HEAD_DIM, s), BF16),
            pltpu.VMEM((hp, nb, HEAD_DIM), F32),
        ],
        compiler_params=_params(2),
        name="moba_attention",
    )(proj, proj, proj)


def _diff_kernel(lam_init, q_ref, k_ref, v_ref, lq1_ref, lk1_ref, lq2_ref, lk2_ref, gs_ref,
                 o_ref, vt_ref):
    blk = MOBA_BLOCK
    nb = q_ref.shape[1] // blk
    hd = HEAD_DIM

    lam = (jnp.exp(jnp.sum(lq1_ref[...] * lk1_ref[...], axis=-1, keepdims=True))
           - jnp.exp(jnp.sum(lq2_ref[...] * lk2_ref[...], axis=-1, keepdims=True))
           + lam_init)

    for n in range(nb):
        rows = slice(n * blk, (n + 1) * blk)
        vt_ref[:, rows] = v_ref[0, rows, :].astype(F32).T.astype(BF16)

    def masked_scores(qi):
        rows = slice(qi * blk, (qi + 1) * blk)
        nk = (qi + 1) * blk
        out = []
        for half in range(2):
            cols = slice(half * hd, (half + 1) * hd)
            s = lax.dot_general(k_ref[0, 0:nk, cols], q_ref[0, rows, cols], NT_DIMS,
                                preferred_element_type=F32)
            blocks = _split_blocks(s, qi + 1)
            blocks[qi] = _causal_mask(blocks[qi])
            out.append(blocks)
        return out

    def attend(qi, blocks1, blocks2):
        rows = slice(qi * blk, (qi + 1) * blk)
        nk = (qi + 1) * blk
        p1, l1 = _prefix_softmax(blocks1)
        p2, l2 = _prefix_softmax(blocks2)
        r1 = 1.0 / l1
        r2 = lam / l2
        w = jnp.concatenate([(a * r1 - b * r2).astype(BF16) for a, b in zip(p1, p2)], axis=0)
        o = jnp.dot(vt_ref[:, 0:nk], w, preferred_element_type=F32)
        ms = jnp.mean(o * o, axis=0, keepdims=True)
        on = (o * lax.rsqrt(ms + NORM_EPS)).T * gs_ref[...]
        o_ref[0, rows, :] = (on * (1.0 - lam_init)).astype(o_ref.dtype)

    ahead = masked_scores(0)
    for qi in range(nb):
        blocks = ahead
        if qi + 1 < nb:
            ahead = masked_scores(qi + 1)
        attend(qi, *blocks)


def _diff_attention(proj, lq1, lk1, lq2, lk2, gs, lam_init):
    b, s, _ = proj.shape
    qoff = 3 * MOBA_WIDTH // DIFF_V_DIM
    head = lambda off: pl.BlockSpec((1, s, DIFF_V_DIM), lambda bi, h: (bi, 0, off + h))
    full = lambda shape: pl.BlockSpec(shape, lambda bi, h: (0,) * len(shape))
    row = lambda v: v.reshape(1, -1).astype(F32)
    return pl.pallas_call(
        functools.partial(_diff_kernel, lam_init),
        grid=(b, DIFF_HEADS),
        in_specs=[head(qoff), head(qoff + DIFF_HEADS), head(qoff + 2 * DIFF_HEADS),
                  full((1, HEAD_DIM)), full((1, HEAD_DIM)), full((1, HEAD_DIM)), full((1, HEAD_DIM)),
                  full((1, DIFF_V_DIM))],
        out_specs=pl.BlockSpec((1, s, DIFF_V_DIM), lambda bi, h: (bi, 0, h)),
        out_shape=jax.ShapeDtypeStruct((b, s, DIFF_HEADS * DIFF_V_DIM), BF16),
        scratch_shapes=[pltpu.VMEM((DIFF_V_DIM, s), BF16)],
        compiler_params=_params(2),
        name="diff_attention",
    )(proj, proj, proj, row(lq1), row(lk1), row(lq2), row(lk2), row(gs))


def _mem_kv_kernel(m_ref, wk_ref, wv_ref, gk_ref, k_ref, v_ref):
    x = m_ref[...]
    k = jnp.dot(x, wk_ref[...].astype(BF16), preferred_element_type=F32)
    for h in range(MEM_HEADS):
        cols = slice(h * HEAD_DIM, (h + 1) * HEAD_DIM)
        kh = k[:, cols]
        ms = jnp.mean(kh * kh, axis=-1, keepdims=True)
        k_ref[:, cols] = (kh * lax.rsqrt(ms + NORM_EPS) * gk_ref[...]).astype(k_ref.dtype)
    v_ref[...] = jnp.dot(x, wv_ref[...].astype(BF16), preferred_element_type=F32).astype(v_ref.dtype)


def _mem_kv(mem_n, wk, wv, gk, tm):
    m, d = mem_n.shape
    return pl.pallas_call(
        _mem_kv_kernel,
        grid=(m // tm,),
        in_specs=[pl.BlockSpec((tm, d), lambda i: (i, 0)),
                  pl.BlockSpec((d, MEM_WIDTH), lambda i: (0, 0)),
                  pl.BlockSpec((d, MEM_WIDTH), lambda i: (0, 0)),
                  pl.BlockSpec((1, HEAD_DIM), lambda i: (0, 0))],
        out_specs=[pl.BlockSpec((tm, MEM_WIDTH), lambda i: (i, 0)),
                   pl.BlockSpec((tm, MEM_WIDTH), lambda i: (i, 0))],
        out_shape=[jax.ShapeDtypeStruct((m, MEM_WIDTH), BF16), jax.ShapeDtypeStruct((m, MEM_WIDTH), BF16)],
        compiler_params=_params(1),
        name="mem_kv",
    )(mem_n, wk, wv, gk.reshape(1, -1).astype(F32))


def _row_rmsnorm(x, g):
    ms = jnp.mean(x * x, axis=-1, keepdims=True)
    return x * lax.rsqrt(ms + NORM_EPS) * g


def _cross_kernel(h_ref, k_ref, v_ref, wq_ref, wo_ref, gq_ref, gin_ref, gout_ref, o_ref, on_ref):
    hn = _row_rmsnorm(h_ref[...], gin_ref[...]).astype(BF16)
    q = jnp.dot(hn, wq_ref[...], preferred_element_type=F32)
    heads = []
    for h in range(MEM_HEADS):
        cols = slice(h * HEAD_DIM, (h + 1) * HEAD_DIM)
        qh = q[:, cols]
        ms = jnp.mean(qh * qh, axis=-1, keepdims=True)
        qh = (qh * lax.rsqrt(ms + NORM_EPS) * gq_ref[...]).astype(BF16)
        s = lax.dot_general(qh, k_ref[0, :, cols], NT_DIMS, preferred_element_type=F32) * ATTN_SCALE
        e = jnp.exp(s - jnp.max(s, axis=-1, keepdims=True))
        p = (e / jnp.sum(e, axis=-1, keepdims=True)).astype(BF16)
        heads.append(jnp.dot(p, v_ref[0, :, cols], preferred_element_type=F32).astype(BF16))
    o = jnp.concatenate(heads, axis=-1)
    out = h_ref[...] + jnp.dot(o, wo_ref[...], preferred_element_type=F32)
    o_ref[...] = out
    on_ref[...] = _row_rmsnorm(out, gout_ref[...]).astype(on_ref.dtype)


def _cross_attention(h, k, v, wq, wo, gq, g_in, g_out, seq, tm):
    m, d = h.shape
    mem_len = k.shape[1]
    per_batch = seq // tm
    row = lambda g: g.reshape(1, -1).astype(F32)
    return pl.pallas_call(
        _cross_kernel,
        grid=(m // tm,),
        in_specs=[pl.BlockSpec((tm, d), lambda i: (i, 0)),
                  pl.BlockSpec((1, mem_len, MEM_WIDTH), lambda i: (i // per_batch, 0, 0)),
                  pl.BlockSpec((1, mem_len, MEM_WIDTH), lambda i: (i // per_batch, 0, 0)),
                  pl.BlockSpec((d, MEM_WIDTH), lambda i: (0, 0)),
                  pl.BlockSpec((MEM_WIDTH, d), lambda i: (0, 0)),
                  pl.BlockSpec((1, HEAD_DIM), lambda i: (0, 0)),
                  pl.BlockSpec((1, d), lambda i: (0, 0)),
                  pl.BlockSpec((1, d), lambda i: (0, 0))],
        out_specs=[pl.BlockSpec((tm, d), lambda i: (i, 0)), pl.BlockSpec((tm, d), lambda i: (i, 0))],
        out_shape=[jax.ShapeDtypeStruct((m, d), F32), jax.ShapeDtypeStruct((m, d), BF16)],
        compiler_params=_params(1),
        name="cross_attention",
    )(h, k, v, wq, wo, row(gq), row(g_in), row(g_out))


def _layer(h, mem_n_src, l, p):
    b, s, d = h.shape
    m = b * s
    lam_init = 0.8 - 0.6 * math.exp(-0.3 * l)
    tables = _rope_tables(s)
    bf = lambda w: w.astype(BF16)
    h2d = h.reshape(m, d)

    xn = _rmsnorm(h2d, p["norm_mix_g"][l], 256)
    proj = _in_projection(xn, p["w_in"][l], _head_gains(p, l), tables, s, 1024, 512).reshape(b, s, IN_WIDTH)
    out_a = _moba_attention(proj)
    out_b = _diff_attention(proj, p["lam_q1"][l], p["lam_k1"][l], p["lam_q2"][l], p["lam_k2"][l],
                            p["diff_subln_g"][l], lam_init)
    h2d = _out_projection(out_a.reshape(m, -1), out_b.reshape(m, -1), p["w_out"][l], h2d, 1024, 512)

    mem_len = mem_n_src.shape[1]
    mem_n = _rmsnorm(mem_n_src.reshape(b * mem_len, d), p["norm_mem_g"][l], 256)
    mk, mv = _mem_kv(mem_n, p["w_mk"][l], p["w_mv"][l], p["k_norm_m"][l], 256)
    h2d, hn = _cross_attention(h2d, mk.reshape(b, mem_len, -1), mv.reshape(b, mem_len, -1),
                               bf(p["w_mq"][l]), bf(p["w_mo"][l]), p["q_norm_m"][l],
                               p["norm_cross_g"][l], p["norm_ffn_g"][l], s, 256)

    act = _swiglu_up(hn, p["w_gate"][l], p["w_up"][l], 2048, 256)
    h2d = _matmul_residual(act, bf(p["w_down"][l]), h2d, 512, 512)
    return h2d.reshape(b, s, d)


def kernel(x, mem, norm_mix_g, w_in, q_norm_a, k_norm_a, q_norm_b, k_norm_b, lam_q1, lam_k1, lam_q2, lam_k2, diff_subln_g, w_out, norm_cross_g, norm_mem_g, w_mq, w_mk, w_mv, w_mo, q_norm_m, k_norm_m, norm_ffn_g, w_gate, w_up, w_down):
    p = dict(norm_mix_g=norm_mix_g, w_in=w_in, q_norm_a=q_norm_a, k_norm_a=k_norm_a, q_norm_b=q_norm_b,
             k_norm_b=k_norm_b, lam_q1=lam_q1, lam_k1=lam_k1, lam_q2=lam_q2, lam_k2=lam_k2,
             diff_subln_g=diff_subln_g, w_out=w_out, norm_cross_g=norm_cross_g, norm_mem_g=norm_mem_g,
             w_mq=w_mq, w_mk=w_mk, w_mv=w_mv, w_mo=w_mo, q_norm_m=q_norm_m, k_norm_m=k_norm_m,
             norm_ffn_g=norm_ffn_g, w_gate=w_gate, w_up=w_up, w_down=w_down)
    h = x
    for l in range(w_in.shape[0]):
        h = _layer(h, mem, l, p)
    return h
```

```python
import functools
import math

import jax
import jax.numpy as jnp
from jax import lax
from jax.experimental import pallas as pl
from jax.experimental.pallas import tpu as pltpu

F32 = jnp.float32
BF16 = jnp.bfloat16

D_MODEL = 4096
HEAD_DIM = 128
MOBA_HEADS = 16
DIFF_HEADS = 8
MOBA_WIDTH = MOBA_HEADS * HEAD_DIM
DIFF_V_DIM = 2 * HEAD_DIM
IN_WIDTH = 12288
MOBA_BLOCK = 256
MOBA_TOPK = 3
MOBA_HEADS_PER_STEP = 2
ROPE_THETA = 500000.0
ROT_DIM = HEAD_DIM // 4
MEM_HEADS = 4
MEM_WIDTH = MEM_HEADS * HEAD_DIM
NORM_EPS = 1e-6
ATTN_SCALE = HEAD_DIM ** -0.5
MASKED = -1e30
EXP2_SCALE = ATTN_SCALE * math.log2(math.e)
SCORE_BOUND_MARGIN = 1.03
MAX_USABLE_SCORE_BOUND = 60.0

V7X_VMEM_BYTES = 64 * 1024 * 1024
VMEM_LIMIT = V7X_VMEM_BYTES - 8 * 1024 * 1024

NT_DIMS = (((1,), (1,)), ((), ()))


def _params(n_grid):
    return pltpu.CompilerParams(
        dimension_semantics=("parallel",) * (n_grid - 1) + ("arbitrary",),
        vmem_limit_bytes=VMEM_LIMIT,
    )


def _rmsnorm_kernel(x_ref, g_ref, o_ref):
    x = x_ref[...].astype(F32)
    ms = jnp.mean(x * x, axis=-1, keepdims=True)
    o_ref[...] = (x * lax.rsqrt(ms + NORM_EPS) * g_ref[...]).astype(o_ref.dtype)


def _rmsnorm(x, g, tm):
    m, d = x.shape
    return pl.pallas_call(
        _rmsnorm_kernel,
        grid=(m // tm,),
        in_specs=[pl.BlockSpec((tm, d), lambda i: (i, 0)), pl.BlockSpec((1, d), lambda i: (0, 0))],
        out_specs=pl.BlockSpec((tm, d), lambda i: (i, 0)),
        out_shape=jax.ShapeDtypeStruct((m, d), BF16),
        compiler_params=_params(1),
        name="rmsnorm",
    )(x, g.reshape(1, d).astype(F32))


def _mm_res_kernel(x_ref, w_ref, r_ref, o_ref):
    o_ref[...] = r_ref[...] + jnp.dot(x_ref[...], w_ref[...], preferred_element_type=F32)


def _matmul_residual(x, w, r, tm, tn):
    m, k = x.shape
    n = w.shape[1]
    return pl.pallas_call(
        _mm_res_kernel,
        grid=(m // tm, n // tn),
        in_specs=[
            pl.BlockSpec((tm, k), lambda i, j: (i, 0)),
            pl.BlockSpec((k, tn), lambda i, j: (0, j)),
            pl.BlockSpec((tm, tn), lambda i, j: (i, j)),
        ],
        out_specs=pl.BlockSpec((tm, tn), lambda i, j: (i, j)),
        out_shape=jax.ShapeDtypeStruct((m, n), F32),
        compiler_params=_params(2),
        name="matmul_residual",
    )(x, w, r)


def _mm2_res_kernel(a_ref, b_ref, wa_ref, wb_ref, r_ref, o_ref):
    acc = jnp.dot(a_ref[...], wa_ref[...].astype(BF16), preferred_element_type=F32)
    acc = acc + jnp.dot(b_ref[...], wb_ref[...].astype(BF16), preferred_element_type=F32)
    o_ref[...] = r_ref[...] + acc


def _out_projection(a, b, w, r, tm, tn):
    m, ka = a.shape
    kb = b.shape[1]
    assert ka == kb and w.shape[0] == ka + kb
    n = w.shape[1]
    return pl.pallas_call(
        _mm2_res_kernel,
        grid=(m // tm, n // tn),
        in_specs=[
            pl.BlockSpec((tm, ka), lambda i, j: (i, 0)),
            pl.BlockSpec((tm, kb), lambda i, j: (i, 0)),
            pl.BlockSpec((ka, tn), lambda i, j: (0, j)),
            pl.BlockSpec((kb, tn), lambda i, j: (1, j)),
            pl.BlockSpec((tm, tn), lambda i, j: (i, j)),
        ],
        out_specs=pl.BlockSpec((tm, tn), lambda i, j: (i, j)),
        out_shape=jax.ShapeDtypeStruct((m, n), F32),
        compiler_params=_params(2),
        name="out_projection",
    )(a, b, w, w, r)


def _swiglu_kernel(x_ref, wg_ref, wu_ref, o_ref):
    x = x_ref[...]
    g = jnp.dot(x, wg_ref[...].astype(BF16), preferred_element_type=F32)
    u = jnp.dot(x, wu_ref[...].astype(BF16), preferred_element_type=F32)
    o_ref[...] = (jax.nn.silu(g) * u).astype(o_ref.dtype)


def _swiglu_up(x, wg, wu, tm, tn):
    m, k = x.shape
    n = wg.shape[1]
    return pl.pallas_call(
        _swiglu_kernel,
        grid=(m // tm, n // tn),
        in_specs=[
            pl.BlockSpec((tm, k), lambda i, j: (i, 0)),
            pl.BlockSpec((k, tn), lambda i, j: (0, j)),
            pl.BlockSpec((k, tn), lambda i, j: (0, j)),
        ],
        out_specs=pl.BlockSpec((tm, tn), lambda i, j: (i, j)),
        out_shape=jax.ShapeDtypeStruct((m, n), BF16),
        compiler_params=_params(2),
        name="swiglu_up",
    )(x, wg, wu)


def _rope_tables(seq):
    half = ROT_DIM // 2
    pos = jnp.arange(seq, dtype=F32)
    inv = ROPE_THETA ** (-jnp.arange(0, ROT_DIM, 2, dtype=F32) / ROT_DIM)
    ang = pos[:, None] * inv[None, :]
    cos, sin = jnp.cos(ang), jnp.sin(ang)
    rest = HEAD_DIM - ROT_DIM
    c = jnp.concatenate([cos, cos, jnp.ones((seq, rest), F32)], axis=-1)
    sa = jnp.concatenate([-sin, jnp.zeros((seq, HEAD_DIM - half), F32)], axis=-1)
    sb = jnp.concatenate([jnp.zeros((seq, half), F32), sin, jnp.zeros((seq, rest), F32)], axis=-1)
    return c, sa, sb


def _head_gains(p, l):
    ones = jnp.ones((HEAD_DIM,), F32)
    rows = [p["q_norm_a"][l].astype(F32) * EXP2_SCALE, p["k_norm_a"][l].astype(F32), ones,
            p["q_norm_b"][l].astype(F32) * EXP2_SCALE, p["k_norm_b"][l].astype(F32), ones]
    return jnp.stack(rows).reshape(6, 1, HEAD_DIM)


def _norm_rope(x, g, c, sa, sb):
    half = ROT_DIM // 2
    ms = jnp.mean(x * x, axis=-1, keepdims=True)
    y = x * lax.rsqrt(ms + NORM_EPS) * g
    up = pltpu.roll(y, HEAD_DIM - half, 1)
    down = pltpu.roll(y, half, 1)
    return y * c + up * sa + down * sb


def _in_proj_kernel(nj, x_ref, w_ref, g_ref, c_ref, sa_ref, sb_ref, o_ref, acc_ref):
    t = pl.program_id(0)
    tn = o_ref.shape[1]

    @pl.when(t == 0)
    def _():
        acc_ref[...] = jnp.zeros_like(acc_ref)

    def next_tile():
        acc_ref[...] = jnp.dot(x_ref[...], w_ref[...].astype(BF16), preferred_element_type=F32)

    region = ((jnp.maximum(t - 1, 0) % nj) * tn) // MOBA_WIDTH
    is_value = region % 3 == 2

    @pl.when(is_value)
    def _():
        prev = acc_ref[...]
        next_tile()
        o_ref[...] = prev.astype(o_ref.dtype)

    @pl.when(jnp.logical_not(is_value))
    def _():
        prev = acc_ref[...]
        next_tile()
        for h in range(tn // HEAD_DIM):
            cols = slice(h * HEAD_DIM, (h + 1) * HEAD_DIM)
            out = _norm_rope(prev[:, cols], g_ref[0], c_ref[...], sa_ref[...], sb_ref[...])
            o_ref[:, cols] = out.astype(o_ref.dtype)


def _in_projection(xn, w, gains, tables, seq, tm, tn):
    m, k = xn.shape
    n = w.shape[1]
    assert MOBA_WIDTH % tn == 0 and seq % tm == 0
    nj = n // tn
    tiles = (m // tm) * nj
    mm = lambda t: jnp.minimum(t, tiles - 1)
    ep = lambda t: jnp.maximum(t - 1, 0)
    region = lambda t: ((ep(t) % nj) * tn) // MOBA_WIDTH
    table_spec = pl.BlockSpec((tm, HEAD_DIM), lambda t: ((ep(t) // nj) % (seq // tm), 0))
    return pl.pallas_call(
        functools.partial(_in_proj_kernel, nj),
        grid=(tiles + 1,),
        in_specs=[pl.BlockSpec((tm, k), lambda t: (mm(t) // nj, 0)),
                  pl.BlockSpec((k, tn), lambda t: (0, mm(t) % nj)),
                  pl.BlockSpec((1, 1, HEAD_DIM), lambda t: (region(t), 0, 0)),
                  table_spec, table_spec, table_spec],
        out_specs=pl.BlockSpec((tm, tn), lambda t: (ep(t) // nj, ep(t) % nj)),
        out_shape=jax.ShapeDtypeStruct((m, n), BF16),
        scratch_shapes=[pltpu.VMEM((tm, tn), F32)],
        compiler_params=_params(1),
        name="in_projection",
    )(xn, w, gains, *tables)


def _causal_mask(s):
    key = lax.broadcasted_iota(jnp.int32, s.shape, 0)
    qry = lax.broadcasted_iota(jnp.int32, s.shape, 1)
    return jnp.where(key <= qry, s, MASKED)


def _prefix_softmax(blocks, shift):
    if shift is None:
        m = jnp.max(blocks[0], axis=0, keepdims=True)
        for sj in blocks[1:]:
            m = jnp.maximum(m, jnp.max(sj, axis=0, keepdims=True))
    else:
        m = shift
    ps, l = [], None
    for sj in blocks:
        pj = jnp.exp2(sj - m)
        lj = jnp.sum(pj, axis=0, keepdims=True)
        l = lj if l is None else l + lj
        ps.append(pj)
    return ps, l


def _score_bound(gq, gk):
    bound = (HEAD_DIM * SCORE_BOUND_MARGIN * jnp.max(jnp.abs(gq.astype(F32))) * EXP2_SCALE
             * jnp.max(jnp.abs(gk.astype(F32))))
    usable = (bound <= MAX_USABLE_SCORE_BOUND).astype(F32)
    return jnp.stack([bound, usable])


def _run_with_score_bound(bound_ref, run):
    usable = bound_ref[1] > 0.5

    @pl.when(usable)
    def _():
        run(bound_ref[0])

    @pl.when(jnp.logical_not(usable))
    def _():
        run(None)


def _split_blocks(s, n):
    return [s[j * MOBA_BLOCK:(j + 1) * MOBA_BLOCK, :] for j in range(n)]


def _moba_kernel(bound_ref, q_ref, k_ref, v_ref, o_ref, vt_ref, kmean_ref):
    blk = MOBA_BLOCK
    nb = q_ref.shape[1] // blk
    heads = q_ref.shape[2] // HEAD_DIM
    lanes = lambda h: slice(h * HEAD_DIM, (h + 1) * HEAD_DIM)

    for h in range(heads):
        for n in range(nb):
            rows = slice(n * blk, (n + 1) * blk)
            kmean_ref[h, n:n + 1, :] = jnp.mean(k_ref[0, rows, lanes(h)].astype(F32), axis=0, keepdims=True)
            vt_ref[h, :, rows] = v_ref[0, rows, lanes(h)].astype(F32).T.astype(BF16)

    def masked_scores(h, qi):
        rows = slice(qi * blk, (qi + 1) * blk)
        nk = (qi + 1) * blk
        qb = q_ref[0, rows, lanes(h)]
        s = lax.dot_general(k_ref[0, 0:nk, lanes(h)], qb, NT_DIMS, preferred_element_type=F32)
        blocks = _split_blocks(s, qi + 1)
        blocks[qi] = _causal_mask(blocks[qi])

        if qi > MOBA_TOPK:
            gate = lax.dot_general(kmean_ref[h], qb.astype(F32), NT_DIMS, preferred_element_type=F32,
                                   precision=lax.Precision.HIGHEST)
            n_iota = lax.broadcasted_iota(jnp.int32, gate.shape, 0)
            beaten = jnp.zeros(gate.shape, jnp.int32)
            for mth in range(qi):
                gm = gate[mth:mth + 1, :]
                wins = (gm > gate) | ((gm == gate) & (mth < n_iota))
                beaten = beaten + wins.astype(jnp.int32)
            bias = jnp.where(beaten < MOBA_TOPK, 0.0, MASKED)
            for j in range(qi):
                blocks[j] = blocks[j] + bias[j:j + 1, :]
        return blocks

    def softmax(blocks, shift):
        ps, l = _prefix_softmax(blocks, shift)
        return jnp.concatenate([pj.astype(BF16) for pj in ps], axis=0), l

    def weighted_values(h, qi, p, l):
        rows = slice(qi * blk, (qi + 1) * blk)
        acc = jnp.dot(vt_ref[h, :, 0:(qi + 1) * blk], p, preferred_element_type=F32)
        o_ref[0, rows, lanes(h)] = (acc / l).T.astype(o_ref.dtype)

    def run(shift):
        items = [(h, qi) for qi in range(nb) for h in range(heads)]
        ahead = masked_scores(*items[0])
        for idx, item in enumerate(items):
            blocks = ahead
            if idx + 1 < len(items):
                ahead = masked_scores(*items[idx + 1])
            weighted_values(*item, *softmax(blocks, shift))

    _run_with_score_bound(bound_ref, run)


def _moba_attention(proj, bound):
    b, s, _ = proj.shape
    nb = s // MOBA_BLOCK
    hp = MOBA_HEADS_PER_STEP
    width = hp * HEAD_DIM
    head = lambda off: pl.BlockSpec((1, s, width), lambda bi, h: (bi, 0, off + h))
    return pl.pallas_call(
        _moba_kernel,
        grid=(b, MOBA_HEADS // hp),
        in_specs=[pl.BlockSpec(memory_space=pltpu.SMEM),
                  head(0), head(MOBA_HEADS // hp), head(2 * MOBA_HEADS // hp)],
        out_specs=pl.BlockSpec((1, s, width), lambda bi, h: (bi, 0, h)),
        out_shape=jax.ShapeDtypeStruct((b, s, MOBA_WIDTH), BF16),
        scratch_shapes=[
            pltpu.VMEM((hp, HEAD_DIM, s), BF16),
            pltpu.VMEM((hp, nb, HEAD_DIM), F32),
        ],
        compiler_params=_params(2),
        name="moba_attention",
    )(bound, proj, proj, proj)


def _diff_kernel(lam_init, bound_ref, q_ref, k_ref, v_ref, lq1_ref, lk1_ref, lq2_ref, lk2_ref, gs_ref,
                 o_ref, vt_ref):
    blk = MOBA_BLOCK
    nb = q_ref.shape[1] // blk
    hd = HEAD_DIM

    lam = (jnp.exp(jnp.sum(lq1_ref[...] * lk1_ref[...], axis=-1, keepdims=True))
           - jnp.exp(jnp.sum(lq2_ref[...] * lk2_ref[...], axis=-1, keepdims=True))
           + lam_init)

    for n in range(nb):
        rows = slice(n * blk, (n + 1) * blk)
        vt_ref[:, rows] = v_ref[0, rows, :].astype(F32).T.astype(BF16)

    def masked_scores(qi):
        rows = slice(qi * blk, (qi + 1) * blk)
        nk = (qi + 1) * blk
        out = []
        for half in range(2):
            cols = slice(half * hd, (half + 1) * hd)
            s = lax.dot_general(k_ref[0, 0:nk, cols], q_ref[0, rows, cols], NT_DIMS,
                                preferred_element_type=F32)
            blocks = _split_blocks(s, qi + 1)
            blocks[qi] = _causal_mask(blocks[qi])
            out.append(blocks)
        return out

    def attend(qi, blocks1, blocks2, shift):
        rows = slice(qi * blk, (qi + 1) * blk)
        nk = (qi + 1) * blk
        p1, l1 = _prefix_softmax(blocks1, shift)
        p2, l2 = _prefix_softmax(blocks2, shift)
        r1 = 1.0 / l1
        r2 = lam / l2
        w = jnp.concatenate([(a * r1 - b * r2).astype(BF16) for a, b in zip(p1, p2)], axis=0)
        o = jnp.dot(vt_ref[:, 0:nk], w, preferred_element_type=F32)
        ms = jnp.mean(o * o, axis=0, keepdims=True)
        on = (o * lax.rsqrt(ms + NORM_EPS)).T * gs_ref[...]
        o_ref[0, rows, :] = (on * (1.0 - lam_init)).astype(o_ref.dtype)

    def run(shift):
        ahead = masked_scores(0)
        for qi in range(nb):
            blocks = ahead
            if qi + 1 < nb:
                ahead = masked_scores(qi + 1)
            attend(qi, *blocks, shift)

    _run_with_score_bound(bound_ref, run)


def _diff_attention(proj, bound, lq1, lk1, lq2, lk2, gs, lam_init):
    b, s, _ = proj.shape
    qoff = 3 * MOBA_WIDTH // DIFF_V_DIM
    head = lambda off: pl.BlockSpec((1, s, DIFF_V_DIM), lambda bi, h: (bi, 0, off + h))
    full = lambda shape: pl.BlockSpec(shape, lambda bi, h: (0,) * len(shape))
    row = lambda v: v.reshape(1, -1).astype(F32)
    return pl.pallas_call(
        functools.partial(_diff_kernel, lam_init),
        grid=(b, DIFF_HEADS),
        in_specs=[pl.BlockSpec(memory_space=pltpu.SMEM),
                  head(qoff), head(qoff + DIFF_HEADS), head(qoff + 2 * DIFF_HEADS),
                  full((1, HEAD_DIM)), full((1, HEAD_DIM)), full((1, HEAD_DIM)), full((1, HEAD_DIM)),
                  full((1, DIFF_V_DIM))],
        out_specs=pl.BlockSpec((1, s, DIFF_V_DIM), lambda bi, h: (bi, 0, h)),
        out_shape=jax.ShapeDtypeStruct((b, s, DIFF_HEADS * DIFF_V_DIM), BF16),
        scratch_shapes=[pltpu.VMEM((DIFF_V_DIM, s), BF16)],
        compiler_params=_params(2),
        name="diff_attention",
    )(bound, proj, proj, proj, row(lq1), row(lk1), row(lq2), row(lk2), row(gs))


def _mem_kv_kernel(m_ref, wk_ref, wv_ref, gk_ref, k_ref, v_ref):
    x = m_ref[...]
    k = jnp.dot(x, wk_ref[...].astype(BF16), preferred_element_type=F32)
    for h in range(MEM_HEADS):
        cols = slice(h * HEAD_DIM, (h + 1) * HEAD_DIM)
        kh = k[:, cols]
        ms = jnp.mean(kh * kh, axis=-1, keepdims=True)
        k_ref[:, cols] = (kh * lax.rsqrt(ms + NORM_EPS) * gk_ref[...]).astype(k_ref.dtype)
    v_ref[...] = jnp.dot(x, wv_ref[...].astype(BF16), preferred_element_type=F32).astype(v_ref.dtype)


def _mem_kv(mem_n, wk, wv, gk, tm):
    m, d = mem_n.shape
    return pl.pallas_call(
        _mem_kv_kernel,
        grid=(m // tm,),
        in_specs=[pl.BlockSpec((tm, d), lambda i: (i, 0)),
                  pl.BlockSpec((d, MEM_WIDTH), lambda i: (0, 0)),
                  pl.BlockSpec((d, MEM_WIDTH), lambda i: (0, 0)),
                  pl.BlockSpec((1, HEAD_DIM), lambda i: (0, 0))],
        out_specs=[pl.BlockSpec((tm, MEM_WIDTH), lambda i: (i, 0)),
                   pl.BlockSpec((tm, MEM_WIDTH), lambda i: (i, 0))],
        out_shape=[jax.ShapeDtypeStruct((m, MEM_WIDTH), BF16), jax.ShapeDtypeStruct((m, MEM_WIDTH), BF16)],
        compiler_params=_params(1),
        name="mem_kv",
    )(mem_n, wk, wv, gk.reshape(1, -1).astype(F32))


def _row_rmsnorm(x, g):
    ms = jnp.mean(x * x, axis=-1, keepdims=True)
    return x * lax.rsqrt(ms + NORM_EPS) * g


def _cross_kernel(h_ref, k_ref, v_ref, wq_ref, wo_ref, gq_ref, gin_ref, gout_ref, o_ref, on_ref):
    hn = _row_rmsnorm(h_ref[...], gin_ref[...]).astype(BF16)
    q = jnp.dot(hn, wq_ref[...], preferred_element_type=F32)
    heads = []
    for h in range(MEM_HEADS):
        cols = slice(h * HEAD_DIM, (h + 1) * HEAD_DIM)
        qh = q[:, cols]
        ms = jnp.mean(qh * qh, axis=-1, keepdims=True)
        qh = (qh * lax.rsqrt(ms + NORM_EPS) * gq_ref[...]).astype(BF16)
        s = lax.dot_general(qh, k_ref[0, :, cols], NT_DIMS, preferred_element_type=F32) * ATTN_SCALE
        e = jnp.exp(s - jnp.max(s, axis=-1, keepdims=True))
        p = (e / jnp.sum(e, axis=-1, keepdims=True)).astype(BF16)
        heads.append(jnp.dot(p, v_ref[0, :, cols], preferred_element_type=F32).astype(BF16))
    o = jnp.concatenate(heads, axis=-1)
    out = h_ref[...] + jnp.dot(o, wo_ref[...], preferred_element_type=F32)
    o_ref[...] = out
    on_ref[...] = _row_rmsnorm(out, gout_ref[...]).astype(on_ref.dtype)


def _cross_attention(h, k, v, wq, wo, gq, g_in, g_out, seq, tm):
    m, d = h.shape
    mem_len = k.shape[1]
    per_batch = seq // tm
    row = lambda g: g.reshape(1, -1).astype(F32)
    return pl.pallas_call(
        _cross_kernel,
        grid=(m // tm,),
        in_specs=[pl.BlockSpec((tm, d), lambda i: (i, 0)),
                  pl.BlockSpec((1, mem_len, MEM_WIDTH), lambda i: (i // per_batch, 0, 0)),
                  pl.BlockSpec((1, mem_len, MEM_WIDTH), lambda i: (i // per_batch, 0, 0)),
                  pl.BlockSpec((d, MEM_WIDTH), lambda i: (0, 0), pipeline_mode=pl.Buffered(1)),
                  pl.BlockSpec((MEM_WIDTH, d), lambda i: (0, 0), pipeline_mode=pl.Buffered(1)),
                  pl.BlockSpec((1, HEAD_DIM), lambda i: (0, 0)),
                  pl.BlockSpec((1, d), lambda i: (0, 0)),
                  pl.BlockSpec((1, d), lambda i: (0, 0))],
        out_specs=[pl.BlockSpec((tm, d), lambda i: (i, 0)), pl.BlockSpec((tm, d), lambda i: (i, 0))],
        out_shape=[jax.ShapeDtypeStruct((m, d), F32), jax.ShapeDtypeStruct((m, d), BF16)],
        compiler_params=_params(1),
        name="cross_attention",
    )(h, k, v, wq, wo, row(gq), row(g_in), row(g_out))


def _layer(h, mem_n_src, l, p):
    b, s, d = h.shape
    m = b * s
    lam_init = 0.8 - 0.6 * math.exp(-0.3 * l)
    tables = _rope_tables(s)
    bf = lambda w: w.astype(BF16)
    h2d = h.reshape(m, d)

    xn = _rmsnorm(h2d, p["norm_mix_g"][l], 256)
    proj = _in_projection(xn, p["w_in"][l], _head_gains(p, l), tables, s, 1024, 512).reshape(b, s, IN_WIDTH)
    out_a = _moba_attention(proj, _score_bound(p["q_norm_a"][l], p["k_norm_a"][l]))
    out_b = _diff_attention(proj, _score_bound(p["q_norm_b"][l], p["k_norm_b"][l]), p["lam_q1"][l], p["lam_k1"][l], p["lam_q2"][l], p["lam_k2"][l],
                            p["diff_subln_g"][l], lam_init)
    h2d = _out_projection(out_a.reshape(m, -1), out_b.reshape(m, -1), p["w_out"][l], h2d, 1024, 512)

    mem_len = mem_n_src.shape[1]
    mem_n = _rmsnorm(mem_n_src.reshape(b * mem_len, d), p["norm_mem_g"][l], 256)
    mk, mv = _mem_kv(mem_n, p["w_mk"][l], p["w_mv"][l], p["k_norm_m"][l], 256)
    h2d, hn = _cross_attention(h2d, mk.reshape(b, mem_len, -1), mv.reshape(b, mem_len, -1),
                               bf(p["w_mq"][l]), bf(p["w_mo"][l]), p["q_norm_m"][l],
                               p["norm_cross_g"][l], p["norm_ffn_g"][l], s, 512)

    act = _swiglu_up(hn, p["w_gate"][l], p["w_up"][l], 2048, 256)
    h2d = _matmul_residual(act, bf(p["w_down"][l]), h2d, 512, 512)
    return h2d.reshape(b, s, d)


def kernel(x, mem, norm_mix_g, w_in, q_norm_a, k_norm_a, q_norm_b, k_norm_b, lam_q1, lam_k1, lam_q2, lam_k2, diff_subln_g, w_out, norm_cross_g, norm_mem_g, w_mq, w_mk, w_mv, w_mo, q_norm_m, k_norm_m, norm_ffn_g, w_gate, w_up, w_down):
    p = dict(norm_mix_g=norm_mix_g, w_in=w_in, q_norm_a=q_norm_a, k_norm_a=k_norm_a, q_norm_b=q_norm_b,
             k_norm_b=k_norm_b, lam_q1=lam_q1, lam_k1=lam_k1, lam_q2=lam_q2, lam_k2=lam_k2,
             diff_subln_g=diff_subln_g, w_out=w_out, norm_cross_g=norm_cross_g, norm_mem_g=norm_mem_g,
             w_mq=w_mq, w_mk=w_mk, w_mv=w_mv, w_mo=w_mo, q_norm_m=q_norm_m, k_norm_m=k_norm_m,
             norm_ffn_g=norm_ffn_g, w_gate=w_gate, w_up=w_up, w_down=w_down)
    h = x
    for l in range(w_in.shape[0]):
        h = _layer(h, mem, l, p)
    return h
```

```python
import functools
import math

import jax
import jax.numpy as jnp
from jax import lax
from jax.experimental import pallas as pl
from jax.experimental.pallas import tpu as pltpu

F32 = jnp.float32
BF16 = jnp.bfloat16

D_MODEL = 4096
HEAD_DIM = 128
MOBA_HEADS = 16
DIFF_HEADS = 8
MOBA_WIDTH = MOBA_HEADS * HEAD_DIM
DIFF_V_DIM = 2 * HEAD_DIM
IN_WIDTH = 12288
MOBA_BLOCK = 256
MOBA_TOPK = 3
MOBA_HEADS_PER_STEP = 2
ROPE_THETA = 500000.0
ROT_DIM = HEAD_DIM // 4
MEM_HEADS = 4
MEM_WIDTH = MEM_HEADS * HEAD_DIM
NORM_EPS = 1e-6
ATTN_SCALE = HEAD_DIM ** -0.5
MASKED = -1e30
EXP2_SCALE = ATTN_SCALE * math.log2(math.e)
SCORE_BOUND_MARGIN = 1.03
MAX_USABLE_SCORE_BOUND = 60.0

V7X_VMEM_BYTES = 64 * 1024 * 1024
VMEM_LIMIT = V7X_VMEM_BYTES - 8 * 1024 * 1024

NT_DIMS = (((1,), (1,)), ((), ()))


def _params(n_grid):
    return pltpu.CompilerParams(
        dimension_semantics=("parallel",) * (n_grid - 1) + ("arbitrary",),
        vmem_limit_bytes=VMEM_LIMIT,
    )


def _rmsnorm_kernel(x_ref, g_ref, o_ref):
    x = x_ref[...].astype(F32)
    ms = jnp.mean(x * x, axis=-1, keepdims=True)
    o_ref[...] = (x * lax.rsqrt(ms + NORM_EPS) * g_ref[...]).astype(o_ref.dtype)


def _rmsnorm(x, g, tm):
    m, d = x.shape
    return pl.pallas_call(
        _rmsnorm_kernel,
        grid=(m // tm,),
        in_specs=[pl.BlockSpec((tm, d), lambda i: (i, 0)), pl.BlockSpec((1, d), lambda i: (0, 0))],
        out_specs=pl.BlockSpec((tm, d), lambda i: (i, 0)),
        out_shape=jax.ShapeDtypeStruct((m, d), BF16),
        compiler_params=_params(1),
        name="rmsnorm",
    )(x, g.reshape(1, d).astype(F32))


def _mm_res_kernel(x_ref, w_ref, r_ref, o_ref):
    o_ref[...] = r_ref[...] + jnp.dot(x_ref[...], w_ref[...], preferred_element_type=F32)


def _matmul_residual(x, w, r, tm, tn):
    m, k = x.shape
    n = w.shape[1]
    return pl.pallas_call(
        _mm_res_kernel,
        grid=(m // tm, n // tn),
        in_specs=[
            pl.BlockSpec((tm, k), lambda i, j: (i, 0)),
            pl.BlockSpec((k, tn), lambda i, j: (0, j)),
            pl.BlockSpec((tm, tn), lambda i, j: (i, j)),
        ],
        out_specs=pl.BlockSpec((tm, tn), lambda i, j: (i, j)),
        out_shape=jax.ShapeDtypeStruct((m, n), F32),
        compiler_params=_params(2),
        name="matmul_residual",
    )(x, w, r)


def _mm2_res_kernel(a_ref, b_ref, wa_ref, wb_ref, r_ref, o_ref):
    acc = jnp.dot(a_ref[...], wa_ref[...].astype(BF16), preferred_element_type=F32)
    acc = acc + jnp.dot(b_ref[...], wb_ref[...].astype(BF16), preferred_element_type=F32)
    o_ref[...] = r_ref[...] + acc


def _out_projection(a, b, w, r, tm, tn):
    m, ka = a.shape
    kb = b.shape[1]
    assert ka == kb and w.shape[0] == ka + kb
    n = w.shape[1]
    return pl.pallas_call(
        _mm2_res_kernel,
        grid=(m // tm, n // tn),
        in_specs=[
            pl.BlockSpec((tm, ka), lambda i, j: (i, 0)),
            pl.BlockSpec((tm, kb), lambda i, j: (i, 0)),
            pl.BlockSpec((ka, tn), lambda i, j: (0, j)),
            pl.BlockSpec((kb, tn), lambda i, j: (1, j)),
            pl.BlockSpec((tm, tn), lambda i, j: (i, j)),
        ],
        out_specs=pl.BlockSpec((tm, tn), lambda i, j: (i, j)),
        out_shape=jax.ShapeDtypeStruct((m, n), F32),
        compiler_params=_params(2),
        name="out_projection",
    )(a, b, w, w, r)


def _swiglu_kernel(x_ref, wg_ref, wu_ref, o_ref):
    x = x_ref[...]
    g = jnp.dot(x, wg_ref[...].astype(BF16), preferred_element_type=F32)
    u = jnp.dot(x, wu_ref[...].astype(BF16), preferred_element_type=F32)
    o_ref[...] = (jax.nn.silu(g) * u).astype(o_ref.dtype)


def _swiglu_up(x, wg, wu, tm, tn):
    m, k = x.shape
    n = wg.shape[1]
    return pl.pallas_call(
        _swiglu_kernel,
        grid=(m // tm, n // tn),
        in_specs=[
            pl.BlockSpec((tm, k), lambda i, j: (i, 0)),
            pl.BlockSpec((k, tn), lambda i, j: (0, j)),
            pl.BlockSpec((k, tn), lambda i, j: (0, j)),
        ],
        out_specs=pl.BlockSpec((tm, tn), lambda i, j: (i, j)),
        out_shape=jax.ShapeDtypeStruct((m, n), BF16),
        compiler_params=_params(2),
        name="swiglu_up",
    )(x, wg, wu)


def _rope_tables(seq):
    half = ROT_DIM // 2
    pos = jnp.arange(seq, dtype=F32)
    inv = ROPE_THETA ** (-jnp.arange(0, ROT_DIM, 2, dtype=F32) / ROT_DIM)
    ang = pos[:, None] * inv[None, :]
    cos, sin = jnp.cos(ang), jnp.sin(ang)
    rest = HEAD_DIM - ROT_DIM
    c = jnp.concatenate([cos, cos, jnp.ones((seq, rest), F32)], axis=-1)
    sa = jnp.concatenate([-sin, jnp.zeros((seq, HEAD_DIM - half), F32)], axis=-1)
    sb = jnp.concatenate([jnp.zeros((seq, half), F32), sin, jnp.zeros((seq, rest), F32)], axis=-1)
    return jnp.stack([c, sa, sb])


def _head_gains(p, l):
    ones = jnp.ones((HEAD_DIM,), F32)
    rows = [p["q_norm_a"][l].astype(F32) * EXP2_SCALE, p["k_norm_a"][l].astype(F32), ones,
            p["q_norm_b"][l].astype(F32) * EXP2_SCALE, p["k_norm_b"][l].astype(F32), ones]
    return jnp.stack(rows).reshape(6, 1, HEAD_DIM)


def _norm_rope(x, g, c, sa, sb):
    half = ROT_DIM // 2
    ms = jnp.mean(x * x, axis=-1, keepdims=True)
    y = x * lax.rsqrt(ms + NORM_EPS) * g
    up = pltpu.roll(y, HEAD_DIM - half, 1)
    down = pltpu.roll(y, half, 1)
    return y * c + up * sa + down * sb


def _in_proj_kernel(nj, x_ref, w_ref, g_ref, rope_ref, o_ref, acc_ref):
    t = pl.program_id(0)
    tn = o_ref.shape[1]

    @pl.when(t == 0)
    def _():
        acc_ref[...] = jnp.zeros_like(acc_ref)

    def next_tile():
        acc_ref[...] = jnp.dot(x_ref[...], w_ref[...].astype(BF16), preferred_element_type=F32)

    region = ((jnp.maximum(t - 1, 0) % nj) * tn) // MOBA_WIDTH
    is_value = region % 3 == 2

    @pl.when(is_value)
    def _():
        prev = acc_ref[...]
        next_tile()
        o_ref[...] = prev.astype(o_ref.dtype)

    @pl.when(jnp.logical_not(is_value))
    def _():
        prev = acc_ref[...]
        next_tile()
        for h in range(tn // HEAD_DIM):
            cols = slice(h * HEAD_DIM, (h + 1) * HEAD_DIM)
            out = _norm_rope(prev[:, cols], g_ref[0], rope_ref[0], rope_ref[1], rope_ref[2])
            o_ref[:, cols] = out.astype(o_ref.dtype)


def _in_projection(xn, w, gains, tables, seq, tm, tn):
    m, k = xn.shape
    n = w.shape[1]
    assert MOBA_WIDTH % tn == 0 and seq % tm == 0
    nj = n // tn
    tiles = (m // tm) * nj
    mm = lambda t: jnp.minimum(t, tiles - 1)
    ep = lambda t: jnp.maximum(t - 1, 0)
    region = lambda t: ((ep(t) % nj) * tn) // MOBA_WIDTH
    table_spec = pl.BlockSpec((3, tm, HEAD_DIM), lambda t: (0, (ep(t) // nj) % (seq // tm), 0))
    return pl.pallas_call(
        functools.partial(_in_proj_kernel, nj),
        grid=(tiles + 1,),
        in_specs=[pl.BlockSpec((tm, k), lambda t: (mm(t) // nj, 0)),
                  pl.BlockSpec((k, tn), lambda t: (0, mm(t) % nj)),
                  pl.BlockSpec((1, 1, HEAD_DIM), lambda t: (region(t), 0, 0)),
                  table_spec],
        out_specs=pl.BlockSpec((tm, tn), lambda t: (ep(t) // nj, ep(t) % nj)),
        out_shape=jax.ShapeDtypeStruct((m, n), BF16),
        scratch_shapes=[pltpu.VMEM((tm, tn), F32)],
        compiler_params=_params(1),
        name="in_projection",
    )(xn, w, gains, tables)


def _causal_mask(s):
    key = lax.broadcasted_iota(jnp.int32, s.shape, 0)
    qry = lax.broadcasted_iota(jnp.int32, s.shape, 1)
    return jnp.where(key <= qry, s, MASKED)


def _prefix_softmax(blocks, shift):
    if shift is None:
        m = jnp.max(blocks[0], axis=0, keepdims=True)
        for sj in blocks[1:]:
            m = jnp.maximum(m, jnp.max(sj, axis=0, keepdims=True))
    else:
        m = shift
    ps, l = [], None
    for sj in blocks:
        pj = jnp.exp2(sj - m)
        lj = jnp.sum(pj, axis=0, keepdims=True)
        l = lj if l is None else l + lj
        ps.append(pj)
    return ps, l


def _score_bound(gq, gk):
    bound = (HEAD_DIM * SCORE_BOUND_MARGIN * jnp.max(jnp.abs(gq.astype(F32))) * EXP2_SCALE
             * jnp.max(jnp.abs(gk.astype(F32))))
    usable = (bound <= MAX_USABLE_SCORE_BOUND).astype(F32)
    return jnp.stack([bound, usable])


def _run_with_score_bound(bound_ref, run):
    usable = bound_ref[1] > 0.5

    @pl.when(usable)
    def _():
        run(bound_ref[0])

    @pl.when(jnp.logical_not(usable))
    def _():
        run(None)


def _split_blocks(s, n):
    return [s[j * MOBA_BLOCK:(j + 1) * MOBA_BLOCK, :] for j in range(n)]


def _moba_kernel(bound_ref, q_ref, k_ref, v_ref, o_ref, vt_ref, kmean_ref):
    blk = MOBA_BLOCK
    nb = q_ref.shape[1] // blk
    heads = q_ref.shape[2] // HEAD_DIM
    lanes = lambda h: slice(h * HEAD_DIM, (h + 1) * HEAD_DIM)

    for h in range(heads):
        for n in range(nb):
            rows = slice(n * blk, (n + 1) * blk)
            kmean_ref[h, n:n + 1, :] = jnp.mean(k_ref[0, rows, lanes(h)].astype(F32), axis=0, keepdims=True)
            vt_ref[h, :, rows] = v_ref[0, rows, lanes(h)].astype(F32).T.astype(BF16)

    def masked_scores(h, qi):
        rows = slice(qi * blk, (qi + 1) * blk)
        nk = (qi + 1) * blk
        qb = q_ref[0, rows, lanes(h)]
        s = lax.dot_general(k_ref[0, 0:nk, lanes(h)], qb, NT_DIMS, preferred_element_type=F32)
        blocks = _split_blocks(s, qi + 1)
        blocks[qi] = _causal_mask(blocks[qi])

        if qi > MOBA_TOPK:
            gate = lax.dot_general(kmean_ref[h], qb.astype(F32), NT_DIMS, preferred_element_type=F32,
                                   precision=lax.Precision.HIGHEST)
            n_iota = lax.broadcasted_iota(jnp.int32, gate.shape, 0)
            beaten = jnp.zeros(gate.shape, jnp.int32)
            for mth in range(qi):
                gm = gate[mth:mth + 1, :]
                wins = (gm > gate) | ((gm == gate) & (mth < n_iota))
                beaten = beaten + wins.astype(jnp.int32)
            bias = jnp.where(beaten < MOBA_TOPK, 0.0, MASKED)
            for j in range(qi):
                blocks[j] = blocks[j] + bias[j:j + 1, :]
        return blocks

    def softmax(blocks, shift):
        ps, l = _prefix_softmax(blocks, shift)
        return jnp.concatenate([pj.astype(BF16) for pj in ps], axis=0), l

    def weighted_values(h, qi, p, l):
        rows = slice(qi * blk, (qi + 1) * blk)
        acc = jnp.dot(vt_ref[h, :, 0:(qi + 1) * blk], p, preferred_element_type=F32)
        o_ref[0, rows, lanes(h)] = (acc / l).T.astype(o_ref.dtype)

    def run(shift):
        items = [(h, qi) for qi in range(nb) for h in range(heads)]
        ahead = masked_scores(*items[0])
        for idx, item in enumerate(items):
            blocks = ahead
            if idx + 1 < len(items):
                ahead = masked_scores(*items[idx + 1])
            weighted_values(*item, *softmax(blocks, shift))

    _run_with_score_bound(bound_ref, run)


def _moba_attention(proj, bound):
    b, s, _ = proj.shape
    nb = s // MOBA_BLOCK
    hp = MOBA_HEADS_PER_STEP
    width = hp * HEAD_DIM
    head = lambda off: pl.BlockSpec((1, s, width), lambda bi, h: (bi, 0, off + h))
    return pl.pallas_call(
        _moba_kernel,
        grid=(b, MOBA_HEADS // hp),
        in_specs=[pl.BlockSpec(memory_space=pltpu.SMEM),
                  head(0), head(MOBA_HEADS // hp), head(2 * MOBA_HEADS // hp)],
        out_specs=pl.BlockSpec((1, s, width), lambda bi, h: (bi, 0, h)),
        out_shape=jax.ShapeDtypeStruct((b, s, MOBA_WIDTH), BF16),
        scratch_shapes=[
            pltpu.VMEM((hp, HEAD_DIM, s), BF16),
            pltpu.VMEM((hp, nb, HEAD_DIM), F32),
        ],
        compiler_params=_params(2),
        name="moba_attention",
    )(bound, proj, proj, proj)


def _diff_kernel(lam_init, bound_ref, q_ref, k_ref, v_ref, lq1_ref, lk1_ref, lq2_ref, lk2_ref, gs_ref,
                 o_ref, vt_ref):
    blk = MOBA_BLOCK
    nb = q_ref.shape[1] // blk
    hd = HEAD_DIM

    lam = (jnp.exp(jnp.sum(lq1_ref[...] * lk1_ref[...], axis=-1, keepdims=True))
           - jnp.exp(jnp.sum(lq2_ref[...] * lk2_ref[...], axis=-1, keepdims=True))
           + lam_init)

    for n in range(nb):
        rows = slice(n * blk, (n + 1) * blk)
        vt_ref[:, rows] = v_ref[0, rows, :].astype(F32).T.astype(BF16)

    def masked_scores(qi):
        rows = slice(qi * blk, (qi + 1) * blk)
        nk = (qi + 1) * blk
        out = []
        for half in range(2):
            cols = slice(half * hd, (half + 1) * hd)
            s = lax.dot_general(k_ref[0, 0:nk, cols], q_ref[0, rows, cols], NT_DIMS,
                                preferred_element_type=F32)
            blocks = _split_blocks(s, qi + 1)
            blocks[qi] = _causal_mask(blocks[qi])
            out.append(blocks)
        return out

    def attend(qi, blocks1, blocks2, shift):
        rows = slice(qi * blk, (qi + 1) * blk)
        nk = (qi + 1) * blk
        p1, l1 = _prefix_softmax(blocks1, shift)
        p2, l2 = _prefix_softmax(blocks2, shift)
        r1 = 1.0 / l1
        r2 = lam / l2
        w = jnp.concatenate([(a * r1 - b * r2).astype(BF16) for a, b in zip(p1, p2)], axis=0)
        o = jnp.dot(vt_ref[:, 0:nk], w, preferred_element_type=F32)
        ms = jnp.mean(o * o, axis=0, keepdims=True)
        on = (o * lax.rsqrt(ms + NORM_EPS)).T * gs_ref[...]
        o_ref[0, rows, :] = (on * (1.0 - lam_init)).astype(o_ref.dtype)

    def run(shift):
        ahead = masked_scores(0)
        for qi in range(nb):
            blocks = ahead
            if qi + 1 < nb:
                ahead = masked_scores(qi + 1)
            attend(qi, *blocks, shift)

    _run_with_score_bound(bound_ref, run)


def _diff_attention(proj, bound, lq1, lk1, lq2, lk2, gs, lam_init):
    b, s, _ = proj.shape
    qoff = 3 * MOBA_WIDTH // DIFF_V_DIM
    head = lambda off: pl.BlockSpec((1, s, DIFF_V_DIM), lambda bi, h: (bi, 0, off + h))
    full = lambda shape: pl.BlockSpec(shape, lambda bi, h: (0,) * len(shape))
    row = lambda v: v.reshape(1, -1).astype(F32)
    return pl.pallas_call(
        functools.partial(_diff_kernel, lam_init),
        grid=(b, DIFF_HEADS),
        in_specs=[pl.BlockSpec(memory_space=pltpu.SMEM),
                  head(qoff), head(qoff + DIFF_HEADS), head(qoff + 2 * DIFF_HEADS),
                  full((1, HEAD_DIM)), full((1, HEAD_DIM)), full((1, HEAD_DIM)), full((1, HEAD_DIM)),
                  full((1, DIFF_V_DIM))],
        out_specs=pl.BlockSpec((1, s, DIFF_V_DIM), lambda bi, h: (bi, 0, h)),
        out_shape=jax.ShapeDtypeStruct((b, s, DIFF_HEADS * DIFF_V_DIM), BF16),
        scratch_shapes=[pltpu.VMEM((DIFF_V_DIM, s), BF16)],
        compiler_params=_params(2),
        name="diff_attention",
    )(bound, proj, proj, proj, row(lq1), row(lk1), row(lq2), row(lk2), row(gs))


def _mem_kv_kernel(m_ref, wk_ref, wv_ref, gk_ref, k_ref, v_ref):
    x = m_ref[...]
    k = jnp.dot(x, wk_ref[...].astype(BF16), preferred_element_type=F32)
    for h in range(MEM_HEADS):
        cols = slice(h * HEAD_DIM, (h + 1) * HEAD_DIM)
        kh = k[:, cols]
        ms = jnp.mean(kh * kh, axis=-1, keepdims=True)
        k_ref[:, cols] = (kh * lax.rsqrt(ms + NORM_EPS) * gk_ref[...]).astype(k_ref.dtype)
    v_ref[...] = jnp.dot(x, wv_ref[...].astype(BF16), preferred_element_type=F32).astype(v_ref.dtype)


def _mem_kv(mem_n, wk, wv, gk, tm):
    m, d = mem_n.shape
    return pl.pallas_call(
        _mem_kv_kernel,
        grid=(m // tm,),
        in_specs=[pl.BlockSpec((tm, d), lambda i: (i, 0)),
                  pl.BlockSpec((d, MEM_WIDTH), lambda i: (0, 0)),
                  pl.BlockSpec((d, MEM_WIDTH), lambda i: (0, 0)),
                  pl.BlockSpec((1, HEAD_DIM), lambda i: (0, 0))],
        out_specs=[pl.BlockSpec((tm, MEM_WIDTH), lambda i: (i, 0)),
                   pl.BlockSpec((tm, MEM_WIDTH), lambda i: (i, 0))],
        out_shape=[jax.ShapeDtypeStruct((m, MEM_WIDTH), BF16), jax.ShapeDtypeStruct((m, MEM_WIDTH), BF16)],
        compiler_params=_params(1),
        name="mem_kv",
    )(mem_n, wk, wv, gk.reshape(1, -1).astype(F32))


def _row_rmsnorm(x, g):
    ms = jnp.mean(x * x, axis=-1, keepdims=True)
    return x * lax.rsqrt(ms + NORM_EPS) * g


def _cross_kernel(h_ref, k_ref, v_ref, wq_ref, wo_ref, gq_ref, gin_ref, gout_ref, o_ref, on_ref):
    hn = _row_rmsnorm(h_ref[...], gin_ref[...]).astype(BF16)
    q = jnp.dot(hn, wq_ref[...], preferred_element_type=F32)
    heads = []
    for h in range(MEM_HEADS):
        cols = slice(h * HEAD_DIM, (h + 1) * HEAD_DIM)
        qh = q[:, cols]
        ms = jnp.mean(qh * qh, axis=-1, keepdims=True)
        qh = (qh * lax.rsqrt(ms + NORM_EPS) * gq_ref[...]).astype(BF16)
        s = lax.dot_general(qh, k_ref[0, :, cols], NT_DIMS, preferred_element_type=F32) * ATTN_SCALE
        e = jnp.exp(s - jnp.max(s, axis=-1, keepdims=True))
        p = (e / jnp.sum(e, axis=-1, keepdims=True)).astype(BF16)
        heads.append(jnp.dot(p, v_ref[0, :, cols], preferred_element_type=F32).astype(BF16))
    o = jnp.concatenate(heads, axis=-1)
    out = h_ref[...] + jnp.dot(o, wo_ref[...], preferred_element_type=F32)
    o_ref[...] = out
    on_ref[...] = _row_rmsnorm(out, gout_ref[...]).astype(on_ref.dtype)


def _cross_attention(h, k, v, wq, wo, gq, g_in, g_out, seq, tm):
    m, d = h.shape
    mem_len = k.shape[1]
    per_batch = seq // tm
    row = lambda g: g.reshape(1, -1).astype(F32)
    return pl.pallas_call(
        _cross_kernel,
        grid=(m // tm,),
        in_specs=[pl.BlockSpec((tm, d), lambda i: (i, 0)),
                  pl.BlockSpec((1, mem_len, MEM_WIDTH), lambda i: (i // per_batch, 0, 0)),
                  pl.BlockSpec((1, mem_len, MEM_WIDTH), lambda i: (i // per_batch, 0, 0)),
                  pl.BlockSpec((d, MEM_WIDTH), lambda i: (0, 0), pipeline_mode=pl.Buffered(1)),
                  pl.BlockSpec((MEM_WIDTH, d), lambda i: (0, 0), pipeline_mode=pl.Buffered(1)),
                  pl.BlockSpec((1, HEAD_DIM), lambda i: (0, 0)),
                  pl.BlockSpec((1, d), lambda i: (0, 0)),
                  pl.BlockSpec((1, d), lambda i: (0, 0))],
        out_specs=[pl.BlockSpec((tm, d), lambda i: (i, 0)), pl.BlockSpec((tm, d), lambda i: (i, 0))],
        out_shape=[jax.ShapeDtypeStruct((m, d), F32), jax.ShapeDtypeStruct((m, d), BF16)],
        compiler_params=_params(1),
        name="cross_attention",
    )(h, k, v, wq, wo, row(gq), row(g_in), row(g_out))


def _layer(h, mem_n_src, l, p):
    b, s, d = h.shape
    m = b * s
    lam_init = 0.8 - 0.6 * math.exp(-0.3 * l)
    tables = _rope_tables(s)
    bf = lambda w: w.astype(BF16)
    h2d = h.reshape(m, d)

    xn = _rmsnorm(h2d, p["norm_mix_g"][l], 256)
    proj = _in_projection(xn, p["w_in"][l], _head_gains(p, l), tables, s, 1024, 512).reshape(b, s, IN_WIDTH)
    out_a = _moba_attention(proj, _score_bound(p["q_norm_a"][l], p["k_norm_a"][l]))
    out_b = _diff_attention(proj, _score_bound(p["q_norm_b"][l], p["k_norm_b"][l]), p["lam_q1"][l], p["lam_k1"][l], p["lam_q2"][l], p["lam_k2"][l],
                            p["diff_subln_g"][l], lam_init)
    h2d = _out_projection(out_a.reshape(m, -1), out_b.reshape(m, -1), bf(p["w_out"][l]), h2d, 1024, 1024)

    mem_len = mem_n_src.shape[1]
    mem_n = _rmsnorm(mem_n_src.reshape(b * mem_len, d), p["norm_mem_g"][l], 256)
    mk, mv = _mem_kv(mem_n, p["w_mk"][l], p["w_mv"][l], p["k_norm_m"][l], 256)
    h2d, hn = _cross_attention(h2d, mk.reshape(b, mem_len, -1), mv.reshape(b, mem_len, -1),
                               bf(p["w_mq"][l]), bf(p["w_mo"][l]), p["q_norm_m"][l],
                               p["norm_cross_g"][l], p["norm_ffn_g"][l], s, 512)

    act = _swiglu_up(hn, p["w_gate"][l], p["w_up"][l], 2048, 256)
    h2d = _matmul_residual(act, bf(p["w_down"][l]), h2d, 512, 512)
    return h2d.reshape(b, s, d)


def kernel(x, mem, norm_mix_g, w_in, q_norm_a, k_norm_a, q_norm_b, k_norm_b, lam_q1, lam_k1, lam_q2, lam_k2, diff_subln_g, w_out, norm_cross_g, norm_mem_g, w_mq, w_mk, w_mv, w_mo, q_norm_m, k_norm_m, norm_ffn_g, w_gate, w_up, w_down):
    p = dict(norm_mix_g=norm_mix_g, w_in=w_in, q_norm_a=q_norm_a, k_norm_a=k_norm_a, q_norm_b=q_norm_b,
             k_norm_b=k_norm_b, lam_q1=lam_q1, lam_k1=lam_k1, lam_q2=lam_q2, lam_k2=lam_k2,
             diff_subln_g=diff_subln_g, w_out=w_out, norm_cross_g=norm_cross_g, norm_mem_g=norm_mem_g,
             w_mq=w_mq, w_mk=w_mk, w_mv=w_mv, w_mo=w_mo, q_norm_m=q_norm_m, k_norm_m=k_norm_m,
             norm_ffn_g=norm_ffn_g, w_gate=w_gate, w_up=w_up, w_down=w_down)
    h = x
    for l in range(w_in.shape[0]):
        h = _layer(h, mem, l, p)
    return h
```

```python
import functools
import math

import jax
import jax.numpy as jnp
from jax import lax
from jax.experimental import pallas as pl
from jax.experimental.pallas import tpu as pltpu

F32 = jnp.float32
BF16 = jnp.bfloat16

HEAD_DIM = 128
MOBA_HEADS = 16
DIFF_HEADS = 8
MOBA_WIDTH = MOBA_HEADS * HEAD_DIM
DIFF_V_DIM = 2 * HEAD_DIM
IN_WIDTH = 12288
MOBA_BLOCK = 256
MOBA_TOPK = 3
MOBA_HEADS_PER_STEP = 2
ROPE_THETA = 500000.0
ROT_DIM = HEAD_DIM // 4
MEM_HEADS = 4
MEM_WIDTH = MEM_HEADS * HEAD_DIM
NORM_EPS = 1e-6
ATTN_SCALE = HEAD_DIM ** -0.5
MASKED = -1e30
EXP2_SCALE = ATTN_SCALE * math.log2(math.e)
SCORE_BOUND_MARGIN = 1.03
MAX_USABLE_SCORE_BOUND = 60.0

V7X_VMEM_BYTES = 64 * 1024 * 1024
VMEM_LIMIT = V7X_VMEM_BYTES - 8 * 1024 * 1024

ROW_NORM_TILE = 256
IN_PROJ_TILE = (1024, 512)
OUT_PROJ_TILE = (1024, 1024)
MEM_KV_TILE = 256
CROSS_TILE = 512
SWIGLU_TILE = (2048, 256)
DOWN_PROJ_TILE = (512, 512)

NT_DIMS = (((1,), (1,)), ((), ()))


def _params(n_grid):
    return pltpu.CompilerParams(
        dimension_semantics=("parallel",) * (n_grid - 1) + ("arbitrary",),
        vmem_limit_bytes=VMEM_LIMIT,
    )


def _row_rmsnorm(x, g):
    ms = jnp.mean(x * x, axis=-1, keepdims=True)
    return x * lax.rsqrt(ms + NORM_EPS) * g


def _rmsnorm_kernel(x_ref, g_ref, o_ref):
    o_ref[...] = _row_rmsnorm(x_ref[...].astype(F32), g_ref[...]).astype(o_ref.dtype)


def _rmsnorm(x, g, tm):
    m, d = x.shape
    return pl.pallas_call(
        _rmsnorm_kernel,
        grid=(m // tm,),
        in_specs=[pl.BlockSpec((tm, d), lambda i: (i, 0)), pl.BlockSpec((1, d), lambda i: (0, 0))],
        out_specs=pl.BlockSpec((tm, d), lambda i: (i, 0)),
        out_shape=jax.ShapeDtypeStruct((m, d), BF16),
        compiler_params=_params(1),
        name="rmsnorm",
    )(x, g.reshape(1, d).astype(F32))


def _mm_res_kernel(x_ref, w_ref, r_ref, o_ref):
    o_ref[...] = r_ref[...] + jnp.dot(x_ref[...], w_ref[...], preferred_element_type=F32)


def _matmul_residual(x, w, r, tm, tn):
    m, k = x.shape
    n = w.shape[1]
    return pl.pallas_call(
        _mm_res_kernel,
        grid=(m // tm, n // tn),
        in_specs=[
            pl.BlockSpec((tm, k), lambda i, j: (i, 0)),
            pl.BlockSpec((k, tn), lambda i, j: (0, j)),
            pl.BlockSpec((tm, tn), lambda i, j: (i, j)),
        ],
        out_specs=pl.BlockSpec((tm, tn), lambda i, j: (i, j)),
        out_shape=jax.ShapeDtypeStruct((m, n), F32),
        compiler_params=_params(2),
        name="matmul_residual",
    )(x, w, r)


def _mm2_res_kernel(a_ref, b_ref, wa_ref, wb_ref, r_ref, o_ref):
    acc = jnp.dot(a_ref[...], wa_ref[...].astype(BF16), preferred_element_type=F32)
    acc = acc + jnp.dot(b_ref[...], wb_ref[...].astype(BF16), preferred_element_type=F32)
    o_ref[...] = r_ref[...] + acc


def _out_projection(a, b, w, r, tm, tn):
    m, ka = a.shape
    kb = b.shape[1]
    assert ka == kb and w.shape[0] == ka + kb
    n = w.shape[1]
    return pl.pallas_call(
        _mm2_res_kernel,
        grid=(m // tm, n // tn),
        in_specs=[
            pl.BlockSpec((tm, ka), lambda i, j: (i, 0)),
            pl.BlockSpec((tm, kb), lambda i, j: (i, 0)),
            pl.BlockSpec((ka, tn), lambda i, j: (0, j)),
            pl.BlockSpec((kb, tn), lambda i, j: (1, j)),
            pl.BlockSpec((tm, tn), lambda i, j: (i, j)),
        ],
        out_specs=pl.BlockSpec((tm, tn), lambda i, j: (i, j)),
        out_shape=jax.ShapeDtypeStruct((m, n), F32),
        compiler_params=_params(2),
        name="out_projection",
    )(a, b, w, w, r)


def _swiglu_kernel(x_ref, wg_ref, wu_ref, o_ref):
    x = x_ref[...]
    g = jnp.dot(x, wg_ref[...].astype(BF16), preferred_element_type=F32)
    u = jnp.dot(x, wu_ref[...].astype(BF16), preferred_element_type=F32)
    o_ref[...] = (jax.nn.silu(g) * u).astype(o_ref.dtype)


def _swiglu_up(x, wg, wu, tm, tn):
    m, k = x.shape
    n = wg.shape[1]
    return pl.pallas_call(
        _swiglu_kernel,
        grid=(m // tm, n // tn),
        in_specs=[
            pl.BlockSpec((tm, k), lambda i, j: (i, 0)),
            pl.BlockSpec((k, tn), lambda i, j: (0, j)),
            pl.BlockSpec((k, tn), lambda i, j: (0, j)),
        ],
        out_specs=pl.BlockSpec((tm, tn), lambda i, j: (i, j)),
        out_shape=jax.ShapeDtypeStruct((m, n), BF16),
        compiler_params=_params(2),
        name="swiglu_up",
    )(x, wg, wu)


def _rope_tables(seq):
    half = ROT_DIM // 2
    pos = jnp.arange(seq, dtype=F32)
    inv = ROPE_THETA ** (-jnp.arange(0, ROT_DIM, 2, dtype=F32) / ROT_DIM)
    ang = pos[:, None] * inv[None, :]
    cos, sin = jnp.cos(ang), jnp.sin(ang)
    rest = HEAD_DIM - ROT_DIM
    c = jnp.concatenate([cos, cos, jnp.ones((seq, rest), F32)], axis=-1)
    sa = jnp.concatenate([-sin, jnp.zeros((seq, HEAD_DIM - half), F32)], axis=-1)
    sb = jnp.concatenate([jnp.zeros((seq, half), F32), sin, jnp.zeros((seq, rest), F32)], axis=-1)
    return jnp.stack([c, sa, sb])


def _head_gains(p, l):
    ones = jnp.ones((HEAD_DIM,), F32)
    rows = [p["q_norm_a"][l].astype(F32) * EXP2_SCALE, p["k_norm_a"][l].astype(F32), ones,
            p["q_norm_b"][l].astype(F32) * EXP2_SCALE, p["k_norm_b"][l].astype(F32), ones]
    return jnp.stack(rows).reshape(6, 1, HEAD_DIM)


def _norm_rope(x, g, c, sa, sb):
    half = ROT_DIM // 2
    ms = jnp.mean(x * x, axis=-1, keepdims=True)
    y = x * lax.rsqrt(ms + NORM_EPS) * g
    up = pltpu.roll(y, HEAD_DIM - half, 1)
    down = pltpu.roll(y, half, 1)
    return y * c + up * sa + down * sb


def _in_proj_kernel(nj, x_ref, w_ref, g_ref, rope_ref, o_ref, acc_ref):
    t = pl.program_id(0)
    tn = o_ref.shape[1]

    @pl.when(t == 0)
    def _():
        acc_ref[...] = jnp.zeros_like(acc_ref)

    def next_tile():
        acc_ref[...] = jnp.dot(x_ref[...], w_ref[...].astype(BF16), preferred_element_type=F32)

    region = ((jnp.maximum(t - 1, 0) % nj) * tn) // MOBA_WIDTH
    is_value = region % 3 == 2

    @pl.when(is_value)
    def _():
        prev = acc_ref[...]
        next_tile()
        o_ref[...] = prev.astype(o_ref.dtype)

    @pl.when(jnp.logical_not(is_value))
    def _():
        prev = acc_ref[...]
        next_tile()
        for h in range(tn // HEAD_DIM):
            cols = slice(h * HEAD_DIM, (h + 1) * HEAD_DIM)
            out = _norm_rope(prev[:, cols], g_ref[0], rope_ref[0], rope_ref[1], rope_ref[2])
            o_ref[:, cols] = out.astype(o_ref.dtype)


def _in_projection(xn, w, gains, tables, seq, tm, tn):
    m, k = xn.shape
    n = w.shape[1]
    assert MOBA_WIDTH % tn == 0 and seq % tm == 0
    nj = n // tn
    tiles = (m // tm) * nj
    mm = lambda t: jnp.minimum(t, tiles - 1)
    ep = lambda t: jnp.maximum(t - 1, 0)
    region = lambda t: ((ep(t) % nj) * tn) // MOBA_WIDTH
    table_spec = pl.BlockSpec((3, tm, HEAD_DIM), lambda t: (0, (ep(t) // nj) % (seq // tm), 0))
    return pl.pallas_call(
        functools.partial(_in_proj_kernel, nj),
        grid=(tiles + 1,),
        in_specs=[pl.BlockSpec((tm, k), lambda t: (mm(t) // nj, 0)),
                  pl.BlockSpec((k, tn), lambda t: (0, mm(t) % nj)),
                  pl.BlockSpec((1, 1, HEAD_DIM), lambda t: (region(t), 0, 0)),
                  table_spec],
        out_specs=pl.BlockSpec((tm, tn), lambda t: (ep(t) // nj, ep(t) % nj)),
        out_shape=jax.ShapeDtypeStruct((m, n), BF16),
        scratch_shapes=[pltpu.VMEM((tm, tn), F32)],
        compiler_params=_params(1),
        name="in_projection",
    )(xn, w, gains, tables)


def _causal_mask(s):
    key = lax.broadcasted_iota(jnp.int32, s.shape, 0)
    qry = lax.broadcasted_iota(jnp.int32, s.shape, 1)
    return jnp.where(key <= qry, s, MASKED)


def _prefix_softmax(blocks, shift):
    if shift is None:
        m = jnp.max(blocks[0], axis=0, keepdims=True)
        for sj in blocks[1:]:
            m = jnp.maximum(m, jnp.max(sj, axis=0, keepdims=True))
    else:
        m = shift
    ps, l = [], None
    for sj in blocks:
        pj = jnp.exp2(sj - m)
        lj = jnp.sum(pj, axis=0, keepdims=True)
        l = lj if l is None else l + lj
        ps.append(pj)
    return ps, l


def _score_bound(gq, gk):
    bound = (HEAD_DIM * SCORE_BOUND_MARGIN * jnp.max(jnp.abs(gq.astype(F32))) * EXP2_SCALE
             * jnp.max(jnp.abs(gk.astype(F32))))
    usable = (bound <= MAX_USABLE_SCORE_BOUND).astype(F32)
    return jnp.stack([bound, usable])


def _run_with_score_bound(bound_ref, run):
    usable = bound_ref[1] > 0.5

    @pl.when(usable)
    def _():
        run(bound_ref[0])

    @pl.when(jnp.logical_not(usable))
    def _():
        run(None)


def _split_blocks(s, n):
    return [s[j * MOBA_BLOCK:(j + 1) * MOBA_BLOCK, :] for j in range(n)]


def _moba_kernel(bound_ref, q_ref, k_ref, v_ref, o_ref, vt_ref, kmean_ref):
    blk = MOBA_BLOCK
    nb = q_ref.shape[1] // blk
    heads = q_ref.shape[2] // HEAD_DIM
    lanes = lambda h: slice(h * HEAD_DIM, (h + 1) * HEAD_DIM)

    for h in range(heads):
        for n in range(nb):
            rows = slice(n * blk, (n + 1) * blk)
            kmean_ref[h, n:n + 1, :] = jnp.mean(k_ref[0, rows, lanes(h)].astype(F32), axis=0, keepdims=True)
            vt_ref[h, :, rows] = v_ref[0, rows, lanes(h)].astype(F32).T.astype(BF16)

    def masked_scores(h, qi):
        rows = slice(qi * blk, (qi + 1) * blk)
        nk = (qi + 1) * blk
        qb = q_ref[0, rows, lanes(h)]
        s = lax.dot_general(k_ref[0, 0:nk, lanes(h)], qb, NT_DIMS, preferred_element_type=F32)
        blocks = _split_blocks(s, qi + 1)
        blocks[qi] = _causal_mask(blocks[qi])

        if qi > MOBA_TOPK:
            gate = lax.dot_general(kmean_ref[h], qb.astype(F32), NT_DIMS, preferred_element_type=F32,
                                   precision=lax.Precision.HIGHEST)
            n_iota = lax.broadcasted_iota(jnp.int32, gate.shape, 0)
            beaten = jnp.zeros(gate.shape, jnp.int32)
            for mth in range(qi):
                gm = gate[mth:mth + 1, :]
                wins = (gm > gate) | ((gm == gate) & (mth < n_iota))
                beaten = beaten + wins.astype(jnp.int32)
            bias = jnp.where(beaten < MOBA_TOPK, 0.0, MASKED)
            for j in range(qi):
                blocks[j] = blocks[j] + bias[j:j + 1, :]
        return blocks

    def softmax(blocks, shift):
        ps, l = _prefix_softmax(blocks, shift)
        return jnp.concatenate([pj.astype(BF16) for pj in ps], axis=0), l

    def weighted_values(h, qi, p, l):
        rows = slice(qi * blk, (qi + 1) * blk)
        acc = jnp.dot(vt_ref[h, :, 0:(qi + 1) * blk], p, preferred_element_type=F32)
        o_ref[0, rows, lanes(h)] = (acc / l).T.astype(o_ref.dtype)

    def run(shift):
        items = [(h, qi) for qi in range(nb) for h in range(heads)]
        ahead = masked_scores(*items[0])
        for idx, item in enumerate(items):
            blocks = ahead
            if idx + 1 < len(items):
                ahead = masked_scores(*items[idx + 1])
            weighted_values(*item, *softmax(blocks, shift))

    _run_with_score_bound(bound_ref, run)


def _moba_attention(proj, bound):
    b, s, _ = proj.shape
    nb = s // MOBA_BLOCK
    hp = MOBA_HEADS_PER_STEP
    width = hp * HEAD_DIM
    head = lambda off: pl.BlockSpec((1, s, width), lambda bi, h: (bi, 0, off + h))
    return pl.pallas_call(
        _moba_kernel,
        grid=(b, MOBA_HEADS // hp),
        in_specs=[pl.BlockSpec(memory_space=pltpu.SMEM),
                  head(0), head(MOBA_HEADS // hp), head(2 * MOBA_HEADS // hp)],
        out_specs=pl.BlockSpec((1, s, width), lambda bi, h: (bi, 0, h)),
        out_shape=jax.ShapeDtypeStruct((b, s, MOBA_WIDTH), BF16),
        scratch_shapes=[
            pltpu.VMEM((hp, HEAD_DIM, s), BF16),
            pltpu.VMEM((hp, nb, HEAD_DIM), F32),
        ],
        compiler_params=_params(2),
        name="moba_attention",
    )(bound, proj, proj, proj)


def _diff_kernel(lam_init, bound_ref, q_ref, k_ref, v_ref, lq1_ref, lk1_ref, lq2_ref, lk2_ref, gs_ref,
                 o_ref, vt_ref):
    blk = MOBA_BLOCK
    nb = q_ref.shape[1] // blk
    hd = HEAD_DIM

    lam = (jnp.exp(jnp.sum(lq1_ref[...] * lk1_ref[...], axis=-1, keepdims=True))
           - jnp.exp(jnp.sum(lq2_ref[...] * lk2_ref[...], axis=-1, keepdims=True))
           + lam_init)

    for n in range(nb):
        rows = slice(n * blk, (n + 1) * blk)
        vt_ref[:, rows] = v_ref[0, rows, :].astype(F32).T.astype(BF16)

    def masked_scores(qi):
        rows = slice(qi * blk, (qi + 1) * blk)
        nk = (qi + 1) * blk
        out = []
        for half in range(2):
            cols = slice(half * hd, (half + 1) * hd)
            s = lax.dot_general(k_ref[0, 0:nk, cols], q_ref[0, rows, cols], NT_DIMS,
                                preferred_element_type=F32)
            blocks = _split_blocks(s, qi + 1)
            blocks[qi] = _causal_mask(blocks[qi])
            out.append(blocks)
        return out

    def attend(qi, blocks1, blocks2, shift):
        rows = slice(qi * blk, (qi + 1) * blk)
        nk = (qi + 1) * blk
        p1, l1 = _prefix_softmax(blocks1, shift)
        p2, l2 = _prefix_softmax(blocks2, shift)
        r1 = 1.0 / l1
        r2 = lam / l2
        w = jnp.concatenate([(a * r1 - b * r2).astype(BF16) for a, b in zip(p1, p2)], axis=0)
        o = jnp.dot(vt_ref[:, 0:nk], w, preferred_element_type=F32)
        ms = jnp.mean(o * o, axis=0, keepdims=True)
        on = (o * lax.rsqrt(ms + NORM_EPS)).T * gs_ref[...]
        o_ref[0, rows, :] = (on * (1.0 - lam_init)).astype(o_ref.dtype)

    def run(shift):
        ahead = masked_scores(0)
        for qi in range(nb):
            blocks = ahead
            if qi + 1 < nb:
                ahead = masked_scores(qi + 1)
            attend(qi, *blocks, shift)

    _run_with_score_bound(bound_ref, run)


def _diff_attention(proj, bound, lq1, lk1, lq2, lk2, gs, lam_init):
    b, s, _ = proj.shape
    qoff = 3 * MOBA_WIDTH // DIFF_V_DIM
    head = lambda off: pl.BlockSpec((1, s, DIFF_V_DIM), lambda bi, h: (bi, 0, off + h))
    full = lambda shape: pl.BlockSpec(shape, lambda bi, h: (0,) * len(shape))
    row = lambda v: v.reshape(1, -1).astype(F32)
    return pl.pallas_call(
        functools.partial(_diff_kernel, lam_init),
        grid=(b, DIFF_HEADS),
        in_specs=[pl.BlockSpec(memory_space=pltpu.SMEM),
                  head(qoff), head(qoff + DIFF_HEADS), head(qoff + 2 * DIFF_HEADS),
                  full((1, HEAD_DIM)), full((1, HEAD_DIM)), full((1, HEAD_DIM)), full((1, HEAD_DIM)),
                  full((1, DIFF_V_DIM))],
        out_specs=pl.BlockSpec((1, s, DIFF_V_DIM), lambda bi, h: (bi, 0, h)),
        out_shape=jax.ShapeDtypeStruct((b, s, DIFF_HEADS * DIFF_V_DIM), BF16),
        scratch_shapes=[pltpu.VMEM((DIFF_V_DIM, s), BF16)],
        compiler_params=_params(2),
        name="diff_attention",
    )(bound, proj, proj, proj, row(lq1), row(lk1), row(lq2), row(lk2), row(gs))


def _mem_kv_kernel(m_ref, wk_ref, wv_ref, gm_ref, gk_ref, k_ref, v_ref):
    x = _row_rmsnorm(m_ref[...], gm_ref[...]).astype(BF16)
    k = jnp.dot(x, wk_ref[...].astype(BF16), preferred_element_type=F32)
    for h in range(MEM_HEADS):
        cols = slice(h * HEAD_DIM, (h + 1) * HEAD_DIM)
        k_ref[:, cols] = _row_rmsnorm(k[:, cols], gk_ref[...]).astype(k_ref.dtype)
    v_ref[...] = jnp.dot(x, wv_ref[...].astype(BF16), preferred_element_type=F32).astype(v_ref.dtype)


def _mem_kv(mem, wk, wv, g_mem, gk, tm):
    m, d = mem.shape
    row = lambda g: g.reshape(1, -1).astype(F32)
    return pl.pallas_call(
        _mem_kv_kernel,
        grid=(m // tm,),
        in_specs=[pl.BlockSpec((tm, d), lambda i: (i, 0)),
                  pl.BlockSpec((d, MEM_WIDTH), lambda i: (0, 0)),
                  pl.BlockSpec((d, MEM_WIDTH), lambda i: (0, 0)),
                  pl.BlockSpec((1, d), lambda i: (0, 0)),
                  pl.BlockSpec((1, HEAD_DIM), lambda i: (0, 0))],
        out_specs=[pl.BlockSpec((tm, MEM_WIDTH), lambda i: (i, 0)),
                   pl.BlockSpec((tm, MEM_WIDTH), lambda i: (i, 0))],
        out_shape=[jax.ShapeDtypeStruct((m, MEM_WIDTH), BF16), jax.ShapeDtypeStruct((m, MEM_WIDTH), BF16)],
        compiler_params=_params(1),
        name="mem_kv",
    )(mem, wk, wv, row(g_mem), row(gk))


def _cross_kernel(h_ref, k_ref, v_ref, wq_ref, wo_ref, gq_ref, gin_ref, gout_ref, o_ref, on_ref):
    hn = _row_rmsnorm(h_ref[...], gin_ref[...]).astype(BF16)
    q = jnp.dot(hn, wq_ref[...], preferred_element_type=F32)
    heads = []
    for h in range(MEM_HEADS):
        cols = slice(h * HEAD_DIM, (h + 1) * HEAD_DIM)
        qh = _row_rmsnorm(q[:, cols], gq_ref[...]).astype(BF16)
        s = lax.dot_general(qh, k_ref[0, :, cols], NT_DIMS, preferred_element_type=F32) * ATTN_SCALE
        e = jnp.exp(s - jnp.max(s, axis=-1, keepdims=True))
        p = (e / jnp.sum(e, axis=-1, keepdims=True)).astype(BF16)
        heads.append(jnp.dot(p, v_ref[0, :, cols], preferred_element_type=F32).astype(BF16))
    o = jnp.concatenate(heads, axis=-1)
    out = h_ref[...] + jnp.dot(o, wo_ref[...], preferred_element_type=F32)
    o_ref[...] = out
    on_ref[...] = _row_rmsnorm(out, gout_ref[...]).astype(on_ref.dtype)


def _cross_attention(h, k, v, wq, wo, gq, g_in, g_out, seq, tm):
    m, d = h.shape
    mem_len = k.shape[1]
    per_batch = seq // tm
    row = lambda g: g.reshape(1, -1).astype(F32)
    return pl.pallas_call(
        _cross_kernel,
        grid=(m // tm,),
        in_specs=[pl.BlockSpec((tm, d), lambda i: (i, 0)),
                  pl.BlockSpec((1, mem_len, MEM_WIDTH), lambda i: (i // per_batch, 0, 0)),
                  pl.BlockSpec((1, mem_len, MEM_WIDTH), lambda i: (i // per_batch, 0, 0)),
                  pl.BlockSpec((d, MEM_WIDTH), lambda i: (0, 0), pipeline_mode=pl.Buffered(1)),
                  pl.BlockSpec((MEM_WIDTH, d), lambda i: (0, 0), pipeline_mode=pl.Buffered(1)),
                  pl.BlockSpec((1, HEAD_DIM), lambda i: (0, 0)),
                  pl.BlockSpec((1, d), lambda i: (0, 0)),
                  pl.BlockSpec((1, d), lambda i: (0, 0))],
        out_specs=[pl.BlockSpec((tm, d), lambda i: (i, 0)), pl.BlockSpec((tm, d), lambda i: (i, 0))],
        out_shape=[jax.ShapeDtypeStruct((m, d), F32), jax.ShapeDtypeStruct((m, d), BF16)],
        compiler_params=_params(1),
        name="cross_attention",
    )(h, k, v, wq, wo, row(gq), row(g_in), row(g_out))


def _layer(h, mem, l, p):
    b, s, d = h.shape
    m = b * s
    lam_init = 0.8 - 0.6 * math.exp(-0.3 * l)
    bf = lambda w: w.astype(BF16)
    h2d = h.reshape(m, d)

    xn = _rmsnorm(h2d, p["norm_mix_g"][l], ROW_NORM_TILE)
    proj = _in_projection(xn, p["w_in"][l], _head_gains(p, l), _rope_tables(s), s, *IN_PROJ_TILE)
    proj = proj.reshape(b, s, IN_WIDTH)
    out_a = _moba_attention(proj, _score_bound(p["q_norm_a"][l], p["k_norm_a"][l]))
    out_b = _diff_attention(proj, _score_bound(p["q_norm_b"][l], p["k_norm_b"][l]),
                            p["lam_q1"][l], p["lam_k1"][l], p["lam_q2"][l], p["lam_k2"][l],
                            p["diff_subln_g"][l], lam_init)
    h2d = _out_projection(out_a.reshape(m, -1), out_b.reshape(m, -1), bf(p["w_out"][l]), h2d,
                          *OUT_PROJ_TILE)

    mem_len = mem.shape[1]
    mk, mv = _mem_kv(mem.reshape(b * mem_len, d), p["w_mk"][l], p["w_mv"][l], p["norm_mem_g"][l],
                     p["k_norm_m"][l], MEM_KV_TILE)
    h2d, hn = _cross_attention(h2d, mk.reshape(b, mem_len, -1), mv.reshape(b, mem_len, -1),
                               bf(p["w_mq"][l]), bf(p["w_mo"][l]), p["q_norm_m"][l],
                               p["norm_cross_g"][l], p["norm_ffn_g"][l], s, CROSS_TILE)

    act = _swiglu_up(hn, p["w_gate"][l], p["w_up"][l], *SWIGLU_TILE)
    h2d = _matmul_residual(act, bf(p["w_down"][l]), h2d, *DOWN_PROJ_TILE)
    return h2d.reshape(b, s, d)


def kernel(x, mem, norm_mix_g, w_in, q_norm_a, k_norm_a, q_norm_b, k_norm_b, lam_q1, lam_k1, lam_q2, lam_k2, diff_subln_g, w_out, norm_cross_g, norm_mem_g, w_mq, w_mk, w_mv, w_mo, q_norm_m, k_norm_m, norm_ffn_g, w_gate, w_up, w_down):
    p = dict(norm_mix_g=norm_mix_g, w_in=w_in, q_norm_a=q_norm_a, k_norm_a=k_norm_a, q_norm_b=q_norm_b,
             k_norm_b=k_norm_b, lam_q1=lam_q1, lam_k1=lam_k1, lam_q2=lam_q2, lam_k2=lam_k2,
             diff_subln_g=diff_subln_g, w_out=w_out, norm_cross_g=norm_cross_g, norm_mem_g=norm_mem_g,
             w_mq=w_mq, w_mk=w_mk, w_mv=w_mv, w_mo=w_mo, q_norm_m=q_norm_m, k_norm_m=k_norm_m,
             norm_ffn_g=norm_ffn_g, w_gate=w_gate, w_up=w_up, w_down=w_down)
    h = x
    for l in range(w_in.shape[0]):
        h = _layer(h, mem, l, p)
    return h
```

```python
import functools
import math

import jax
import jax.numpy as jnp
from jax import lax
from jax.experimental import pallas as pl
from jax.experimental.pallas import tpu as pltpu

F32 = jnp.float32
BF16 = jnp.bfloat16

HEAD_DIM = 128
MOBA_HEADS = 16
DIFF_HEADS = 8
MOBA_WIDTH = MOBA_HEADS * HEAD_DIM
DIFF_V_DIM = 2 * HEAD_DIM
IN_WIDTH = 12288
MOBA_BLOCK = 256
MOBA_TOPK = 3
MOBA_HEADS_PER_STEP = 2
ROPE_THETA = 500000.0
ROT_DIM = HEAD_DIM // 4
MEM_HEADS = 4
MEM_WIDTH = MEM_HEADS * HEAD_DIM
NORM_EPS = 1e-6
ATTN_SCALE = HEAD_DIM ** -0.5
MASKED = -1e30
EXP2_SCALE = ATTN_SCALE * math.log2(math.e)
SCORE_BOUND_MARGIN = 1.03
MAX_USABLE_SCORE_BOUND = 60.0

V7X_VMEM_BYTES = 64 * 1024 * 1024
VMEM_LIMIT = V7X_VMEM_BYTES - 8 * 1024 * 1024

NORM_CHUNK = 128
IN_PROJ_TILE = (1024, 512)
OUT_PROJ_TILE = (1024, 1024)
MEM_KV_TILE = 256
CROSS_TILE = 512
SWIGLU_TILE = (2048, 256)
DOWN_PROJ_TILE = (512, 512)

NT_DIMS = (((1,), (1,)), ((), ()))


def _params(n_grid):
    return pltpu.CompilerParams(
        dimension_semantics=("parallel",) * (n_grid - 1) + ("arbitrary",),
        vmem_limit_bytes=VMEM_LIMIT,
    )


def _row_rmsnorm(x, g):
    ms = jnp.mean(x * x, axis=-1, keepdims=True)
    return x * lax.rsqrt(ms + NORM_EPS) * g


def _mm_res_kernel(x_ref, w_ref, r_ref, o_ref):
    o_ref[...] = r_ref[...] + jnp.dot(x_ref[...], w_ref[...], preferred_element_type=F32)


def _matmul_residual(x, w, r, tm, tn):
    m, k = x.shape
    n = w.shape[1]
    return pl.pallas_call(
        _mm_res_kernel,
        grid=(m // tm, n // tn),
        in_specs=[
            pl.BlockSpec((tm, k), lambda i, j: (i, 0)),
            pl.BlockSpec((k, tn), lambda i, j: (0, j)),
            pl.BlockSpec((tm, tn), lambda i, j: (i, j)),
        ],
        out_specs=pl.BlockSpec((tm, tn), lambda i, j: (i, j)),
        out_shape=jax.ShapeDtypeStruct((m, n), F32),
        compiler_params=_params(2),
        name="matmul_residual",
    )(x, w, r)


def _mm2_res_kernel(a_ref, b_ref, wa_ref, wb_ref, r_ref, o_ref):
    acc = jnp.dot(a_ref[...], wa_ref[...].astype(BF16), preferred_element_type=F32)
    acc = acc + jnp.dot(b_ref[...], wb_ref[...].astype(BF16), preferred_element_type=F32)
    o_ref[...] = r_ref[...] + acc


def _out_projection(a, b, w, r, tm, tn):
    m, ka = a.shape
    kb = b.shape[1]
    assert ka == kb and w.shape[0] == ka + kb
    n = w.shape[1]
    return pl.pallas_call(
        _mm2_res_kernel,
        grid=(m // tm, n // tn),
        in_specs=[
            pl.BlockSpec((tm, ka), lambda i, j: (i, 0)),
            pl.BlockSpec((tm, kb), lambda i, j: (i, 0)),
            pl.BlockSpec((ka, tn), lambda i, j: (0, j)),
            pl.BlockSpec((kb, tn), lambda i, j: (1, j)),
            pl.BlockSpec((tm, tn), lambda i, j: (i, j)),
        ],
        out_specs=pl.BlockSpec((tm, tn), lambda i, j: (i, j)),
        out_shape=jax.ShapeDtypeStruct((m, n), F32),
        compiler_params=_params(2),
        name="out_projection",
    )(a, b, w, w, r)


def _swiglu_kernel(x_ref, wg_ref, wu_ref, o_ref):
    x = x_ref[...]
    g = jnp.dot(x, wg_ref[...].astype(BF16), preferred_element_type=F32)
    u = jnp.dot(x, wu_ref[...].astype(BF16), preferred_element_type=F32)
    o_ref[...] = (jax.nn.silu(g) * u).astype(o_ref.dtype)


def _swiglu_up(x, wg, wu, tm, tn):
    m, k = x.shape
    n = wg.shape[1]
    return pl.pallas_call(
        _swiglu_kernel,
        grid=(m // tm, n // tn),
        in_specs=[
            pl.BlockSpec((tm, k), lambda i, j: (i, 0)),
            pl.BlockSpec((k, tn), lambda i, j: (0, j)),
            pl.BlockSpec((k, tn), lambda i, j: (0, j)),
        ],
        out_specs=pl.BlockSpec((tm, tn), lambda i, j: (i, j)),
        out_shape=jax.ShapeDtypeStruct((m, n), BF16),
        compiler_params=_params(2),
        name="swiglu_up",
    )(x, wg, wu)


def _rope_tables(seq):
    half = ROT_DIM // 2
    pos = jnp.arange(seq, dtype=F32)
    inv = ROPE_THETA ** (-jnp.arange(0, ROT_DIM, 2, dtype=F32) / ROT_DIM)
    ang = pos[:, None] * inv[None, :]
    cos, sin = jnp.cos(ang), jnp.sin(ang)
    rest = HEAD_DIM - ROT_DIM
    c = jnp.concatenate([cos, cos, jnp.ones((seq, rest), F32)], axis=-1)
    sa = jnp.concatenate([-sin, jnp.zeros((seq, HEAD_DIM - half), F32)], axis=-1)
    sb = jnp.concatenate([jnp.zeros((seq, half), F32), sin, jnp.zeros((seq, rest), F32)], axis=-1)
    return jnp.stack([c, sa, sb])


def _head_gains(p, l):
    ones = jnp.ones((HEAD_DIM,), F32)
    rows = [p["q_norm_a"][l].astype(F32) * EXP2_SCALE, p["k_norm_a"][l].astype(F32), ones,
            p["q_norm_b"][l].astype(F32) * EXP2_SCALE, p["k_norm_b"][l].astype(F32), ones]
    return jnp.stack(rows).reshape(6, 1, HEAD_DIM)


def _norm_rope(x, g, c, sa, sb):
    half = ROT_DIM // 2
    ms = jnp.mean(x * x, axis=-1, keepdims=True)
    y = x * lax.rsqrt(ms + NORM_EPS) * g
    up = pltpu.roll(y, HEAD_DIM - half, 1)
    down = pltpu.roll(y, half, 1)
    return y * c + up * sa + down * sb


def _in_proj_kernel(nj, tiles, x_ref, gx_ref, w_ref, g_ref, rope_ref, o_ref, xn_ref, acc_ref):
    t = pl.program_id(0)
    tm, tn = o_ref.shape
    chunks = tm // NORM_CHUNK
    u = t - chunks
    tile = jnp.clip(u, 0, tiles - 1)
    row_tile, col_tile = tile // nj, tile % nj

    def normalise_chunk(slot, chunk):
        rows = pl.ds(pl.multiple_of(chunk * NORM_CHUNK, NORM_CHUNK), NORM_CHUNK)
        xn_ref[slot, rows, :] = _row_rmsnorm(x_ref[...], gx_ref[...]).astype(xn_ref.dtype)

    @pl.when(t == 0)
    def _():
        acc_ref[...] = jnp.zeros_like(acc_ref)

    @pl.when(u < 0)
    def _():
        normalise_chunk(0, t)

    def main_step(with_chunk):
        prev = acc_ref[...]
        acc_ref[...] = jnp.dot(xn_ref[row_tile % 2], w_ref[...].astype(BF16), preferred_element_type=F32)
        if with_chunk:
            normalise_chunk((row_tile + 1) % 2, col_tile)
        return prev

    def finish_value(prev):
        o_ref[...] = prev.astype(o_ref.dtype)

    def finish_qk(prev):
        for h in range(tn // HEAD_DIM):
            cols = slice(h * HEAD_DIM, (h + 1) * HEAD_DIM)
            out = _norm_rope(prev[:, cols], g_ref[0], rope_ref[0], rope_ref[1], rope_ref[2])
            o_ref[:, cols] = out.astype(o_ref.dtype)

    region = ((jnp.clip(u - 1, 0, tiles - 1) % nj) * tn) // MOBA_WIDTH
    is_value = region % 3 == 2
    has_chunk = col_tile < chunks
    for value_tile, finish in ((True, finish_value), (False, finish_qk)):
        for with_chunk in (True, False):
            cond = jnp.logical_and(u >= 0, jnp.logical_and(is_value == value_tile, has_chunk == with_chunk))
            pl.when(cond)(lambda finish=finish, with_chunk=with_chunk: finish(main_step(with_chunk)))


def _in_projection(x, gx, w, gains, tables, seq, tm, tn):
    m, k = x.shape
    n = w.shape[1]
    assert MOBA_WIDTH % tn == 0 and seq % tm == 0 and tm % NORM_CHUNK == 0
    nj = n // tn
    tiles = (m // tm) * nj
    chunks = tm // NORM_CHUNK
    assert chunks <= nj
    mm = lambda t: jnp.clip(t - chunks, 0, tiles - 1)
    ep = lambda t: jnp.clip(t - chunks - 1, 0, tiles - 1)
    region = lambda t: ((ep(t) % nj) * tn) // MOBA_WIDTH

    def x_chunk(t):
        ahead = (mm(t) // nj + 1) * chunks + jnp.minimum(mm(t) % nj, chunks - 1)
        return jnp.where(t < chunks, t, jnp.minimum(ahead, m // NORM_CHUNK - 1))

    return pl.pallas_call(
        functools.partial(_in_proj_kernel, nj, tiles),
        grid=(chunks + tiles + 1,),
        in_specs=[pl.BlockSpec((NORM_CHUNK, k), lambda t: (x_chunk(t), 0)),
                  pl.BlockSpec((1, k), lambda t: (0, 0)),
                  pl.BlockSpec((k, tn), lambda t: (0, mm(t) % nj)),
                  pl.BlockSpec((1, 1, HEAD_DIM), lambda t: (region(t), 0, 0)),
                  pl.BlockSpec((3, tm, HEAD_DIM), lambda t: (0, (ep(t) // nj) % (seq // tm), 0))],
        out_specs=pl.BlockSpec((tm, tn), lambda t: (ep(t) // nj, ep(t) % nj)),
        out_shape=jax.ShapeDtypeStruct((m, n), BF16),
        scratch_shapes=[pltpu.VMEM((2, tm, k), BF16),
                        pltpu.VMEM((tm, tn), F32)],
        compiler_params=_params(1),
        name="in_projection",
    )(x, gx.reshape(1, k).astype(F32), w, gains, tables)


def _causal_mask(s):
    key = lax.broadcasted_iota(jnp.int32, s.shape, 0)
    qry = lax.broadcasted_iota(jnp.int32, s.shape, 1)
    return jnp.where(key <= qry, s, MASKED)


def _prefix_softmax(blocks, shift):
    if shift is None:
        m = jnp.max(blocks[0], axis=0, keepdims=True)
        for sj in blocks[1:]:
            m = jnp.maximum(m, jnp.max(sj, axis=0, keepdims=True))
    else:
        m = shift
    ps, l = [], None
    for sj in blocks:
        pj = jnp.exp2(sj - m)
        lj = jnp.sum(pj, axis=0, keepdims=True)
        l = lj if l is None else l + lj
        ps.append(pj)
    return ps, l


def _score_bound(gq, gk):
    bound = (HEAD_DIM * SCORE_BOUND_MARGIN * jnp.max(jnp.abs(gq.astype(F32))) * EXP2_SCALE
             * jnp.max(jnp.abs(gk.astype(F32))))
    usable = (bound <= MAX_USABLE_SCORE_BOUND).astype(F32)
    return jnp.stack([bound, usable])


def _run_with_score_bound(bound_ref, run):
    usable = bound_ref[1] > 0.5

    @pl.when(usable)
    def _():
        run(bound_ref[0])

    @pl.when(jnp.logical_not(usable))
    def _():
        run(None)


def _split_blocks(s, n):
    return [s[j * MOBA_BLOCK:(j + 1) * MOBA_BLOCK, :] for j in range(n)]


def _moba_kernel(bound_ref, q_ref, k_ref, v_ref, o_ref, vt_ref, kmean_ref):
    blk = MOBA_BLOCK
    nb = q_ref.shape[1] // blk
    heads = q_ref.shape[2] // HEAD_DIM
    lanes = lambda h: slice(h * HEAD_DIM, (h + 1) * HEAD_DIM)

    for h in range(heads):
        for n in range(nb):
            rows = slice(n * blk, (n + 1) * blk)
            kmean_ref[h, n:n + 1, :] = jnp.mean(k_ref[0, rows, lanes(h)].astype(F32), axis=0, keepdims=True)
            vt_ref[h, :, rows] = v_ref[0, rows, lanes(h)].astype(F32).T.astype(BF16)

    def masked_scores(h, qi):
        rows = slice(qi * blk, (qi + 1) * blk)
        nk = (qi + 1) * blk
        qb = q_ref[0, rows, lanes(h)]
        s = lax.dot_general(k_ref[0, 0:nk, lanes(h)], qb, NT_DIMS, preferred_element_type=F32)
        blocks = _split_blocks(s, qi + 1)
        blocks[qi] = _causal_mask(blocks[qi])

        if qi > MOBA_TOPK:
            gate = lax.dot_general(kmean_ref[h], qb.astype(F32), NT_DIMS, preferred_element_type=F32,
                                   precision=lax.Precision.HIGHEST)
            n_iota = lax.broadcasted_iota(jnp.int32, gate.shape, 0)
            beaten = jnp.zeros(gate.shape, jnp.int32)
            for mth in range(qi):
                gm = gate[mth:mth + 1, :]
                wins = (gm > gate) | ((gm == gate) & (mth < n_iota))
                beaten = beaten + wins.astype(jnp.int32)
            bias = jnp.where(beaten < MOBA_TOPK, 0.0, MASKED)
            for j in range(qi):
                blocks[j] = blocks[j] + bias[j:j + 1, :]
        return blocks

    def softmax(blocks, shift):
        ps, l = _prefix_softmax(blocks, shift)
        return jnp.concatenate([pj.astype(BF16) for pj in ps], axis=0), l

    def weighted_values(h, qi, p, l):
        rows = slice(qi * blk, (qi + 1) * blk)
        acc = jnp.dot(vt_ref[h, :, 0:(qi + 1) * blk], p, preferred_element_type=F32)
        o_ref[0, rows, lanes(h)] = (acc / l).T.astype(o_ref.dtype)

    def run(shift):
        items = [(h, qi) for qi in range(nb) for h in range(heads)]
        ahead = masked_scores(*items[0])
        for idx, item in enumerate(items):
            blocks = ahead
            if idx + 1 < len(items):
                ahead = masked_scores(*items[idx + 1])
            weighted_values(*item, *softmax(blocks, shift))

    _run_with_score_bound(bound_ref, run)


def _moba_attention(proj, bound):
    b, s, _ = proj.shape
    nb = s // MOBA_BLOCK
    hp = MOBA_HEADS_PER_STEP
    width = hp * HEAD_DIM
    head = lambda off: pl.BlockSpec((1, s, width), lambda bi, h: (bi, 0, off + h))
    return pl.pallas_call(
        _moba_kernel,
        grid=(b, MOBA_HEADS // hp),
        in_specs=[pl.BlockSpec(memory_space=pltpu.SMEM),
                  head(0), head(MOBA_HEADS // hp), head(2 * MOBA_HEADS // hp)],
        out_specs=pl.BlockSpec((1, s, width), lambda bi, h: (bi, 0, h)),
        out_shape=jax.ShapeDtypeStruct((b, s, MOBA_WIDTH), BF16),
        scratch_shapes=[
            pltpu.VMEM((hp, HEAD_DIM, s), BF16),
            pltpu.VMEM((hp, nb, HEAD_DIM), F32),
        ],
        compiler_params=_params(2),
        name="moba_attention",
    )(bound, proj, proj, proj)


def _diff_kernel(lam_init, bound_ref, q_ref, k_ref, v_ref, lq1_ref, lk1_ref, lq2_ref, lk2_ref, gs_ref,
                 o_ref, vt_ref):
    blk = MOBA_BLOCK
    nb = q_ref.shape[1] // blk
    hd = HEAD_DIM

    lam = (jnp.exp(jnp.sum(lq1_ref[...] * lk1_ref[...], axis=-1, keepdims=True))
           - jnp.exp(jnp.sum(lq2_ref[...] * lk2_ref[...], axis=-1, keepdims=True))
           + lam_init)

    for n in range(nb):
        rows = slice(n * blk, (n + 1) * blk)
        vt_ref[:, rows] = v_ref[0, rows, :].astype(F32).T.astype(BF16)

    def masked_scores(qi):
        rows = slice(qi * blk, (qi + 1) * blk)
        nk = (qi + 1) * blk
        out = []
        for half in range(2):
            cols = slice(half * hd, (half + 1) * hd)
            s = lax.dot_general(k_ref[0, 0:nk, cols], q_ref[0, rows, cols], NT_DIMS,
                                preferred_element_type=F32)
            blocks = _split_blocks(s, qi + 1)
            blocks[qi] = _causal_mask(blocks[qi])
            out.append(blocks)
        return out

    def attend(qi, blocks1, blocks2, shift):
        rows = slice(qi * blk, (qi + 1) * blk)
        nk = (qi + 1) * blk
        p1, l1 = _prefix_softmax(blocks1, shift)
        p2, l2 = _prefix_softmax(blocks2, shift)
        r1 = 1.0 / l1
        r2 = lam / l2
        w = jnp.concatenate([(a * r1 - b * r2).astype(BF16) for a, b in zip(p1, p2)], axis=0)
        o = jnp.dot(vt_ref[:, 0:nk], w, preferred_element_type=F32)
        ms = jnp.mean(o * o, axis=0, keepdims=True)
        on = (o * lax.rsqrt(ms + NORM_EPS)).T * gs_ref[...]
        o_ref[0, rows, :] = (on * (1.0 - lam_init)).astype(o_ref.dtype)

    def run(shift):
        ahead = masked_scores(0)
        for qi in range(nb):
            blocks = ahead
            if qi + 1 < nb:
                ahead = masked_scores(qi + 1)
            attend(qi, *blocks, shift)

    _run_with_score_bound(bound_ref, run)


def _diff_attention(proj, bound, lq1, lk1, lq2, lk2, gs, lam_init):
    b, s, _ = proj.shape
    qoff = 3 * MOBA_WIDTH // DIFF_V_DIM
    head = lambda off: pl.BlockSpec((1, s, DIFF_V_DIM), lambda bi, h: (bi, 0, off + h))
    full = lambda shape: pl.BlockSpec(shape, lambda bi, h: (0,) * len(shape))
    row = lambda v: v.reshape(1, -1).astype(F32)
    return pl.pallas_call(
        functools.partial(_diff_kernel, lam_init),
        grid=(b, DIFF_HEADS),
        in_specs=[pl.BlockSpec(memory_space=pltpu.SMEM),
                  head(qoff), head(qoff + DIFF_HEADS), head(qoff + 2 * DIFF_HEADS),
                  full((1, HEAD_DIM)), full((1, HEAD_DIM)), full((1, HEAD_DIM)), full((1, HEAD_DIM)),
                  full((1, DIFF_V_DIM))],
        out_specs=pl.BlockSpec((1, s, DIFF_V_DIM), lambda bi, h: (bi, 0, h)),
        out_shape=jax.ShapeDtypeStruct((b, s, DIFF_HEADS * DIFF_V_DIM), BF16),
        scratch_shapes=[pltpu.VMEM((DIFF_V_DIM, s), BF16)],
        compiler_params=_params(2),
        name="diff_attention",
    )(bound, proj, proj, proj, row(lq1), row(lk1), row(lq2), row(lk2), row(gs))


def _mem_kv_kernel(m_ref, wk_ref, wv_ref, gm_ref, gk_ref, k_ref, v_ref):
    x = _row_rmsnorm(m_ref[...], gm_ref[...]).astype(BF16)
    k = jnp.dot(x, wk_ref[...].astype(BF16), preferred_element_type=F32)
    for h in range(MEM_HEADS):
        cols = slice(h * HEAD_DIM, (h + 1) * HEAD_DIM)
        k_ref[:, cols] = _row_rmsnorm(k[:, cols], gk_ref[...]).astype(k_ref.dtype)
    v_ref[...] = jnp.dot(x, wv_ref[...].astype(BF16), preferred_element_type=F32).astype(v_ref.dtype)


def _mem_kv(mem, wk, wv, g_mem, gk, tm):
    m, d = mem.shape
    row = lambda g: g.reshape(1, -1).astype(F32)
    return pl.pallas_call(
        _mem_kv_kernel,
        grid=(m // tm,),
        in_specs=[pl.BlockSpec((tm, d), lambda i: (i, 0)),
                  pl.BlockSpec((d, MEM_WIDTH), lambda i: (0, 0)),
                  pl.BlockSpec((d, MEM_WIDTH), lambda i: (0, 0)),
                  pl.BlockSpec((1, d), lambda i: (0, 0)),
                  pl.BlockSpec((1, HEAD_DIM), lambda i: (0, 0))],
        out_specs=[pl.BlockSpec((tm, MEM_WIDTH), lambda i: (i, 0)),
                   pl.BlockSpec((tm, MEM_WIDTH), lambda i: (i, 0))],
        out_shape=[jax.ShapeDtypeStruct((m, MEM_WIDTH), BF16), jax.ShapeDtypeStruct((m, MEM_WIDTH), BF16)],
        compiler_params=_params(1),
        name="mem_kv",
    )(mem, wk, wv, row(g_mem), row(gk))


def _cross_kernel(h_ref, k_ref, v_ref, wq_ref, wo_ref, gq_ref, gin_ref, gout_ref, o_ref, on_ref):
    hn = _row_rmsnorm(h_ref[...], gin_ref[...]).astype(BF16)
    q = jnp.dot(hn, wq_ref[...], preferred_element_type=F32)
    heads = []
    for h in range(MEM_HEADS):
        cols = slice(h * HEAD_DIM, (h + 1) * HEAD_DIM)
        qh = _row_rmsnorm(q[:, cols], gq_ref[...]).astype(BF16)
        s = lax.dot_general(qh, k_ref[0, :, cols], NT_DIMS, preferred_element_type=F32) * ATTN_SCALE
        e = jnp.exp(s - jnp.max(s, axis=-1, keepdims=True))
        p = (e / jnp.sum(e, axis=-1, keepdims=True)).astype(BF16)
        heads.append(jnp.dot(p, v_ref[0, :, cols], preferred_element_type=F32).astype(BF16))
    o = jnp.concatenate(heads, axis=-1)
    out = h_ref[...] + jnp.dot(o, wo_ref[...], preferred_element_type=F32)
    o_ref[...] = out
    on_ref[...] = _row_rmsnorm(out, gout_ref[...]).astype(on_ref.dtype)


def _cross_attention(h, k, v, wq, wo, gq, g_in, g_out, seq, tm):
    m, d = h.shape
    mem_len = k.shape[1]
    per_batch = seq // tm
    row = lambda g: g.reshape(1, -1).astype(F32)
    return pl.pallas_call(
        _cross_kernel,
        grid=(m // tm,),
        in_specs=[pl.BlockSpec((tm, d), lambda i: (i, 0)),
                  pl.BlockSpec((1, mem_len, MEM_WIDTH), lambda i: (i // per_batch, 0, 0)),
                  pl.BlockSpec((1, mem_len, MEM_WIDTH), lambda i: (i // per_batch, 0, 0)),
                  pl.BlockSpec((d, MEM_WIDTH), lambda i: (0, 0), pipeline_mode=pl.Buffered(1)),
                  pl.BlockSpec((MEM_WIDTH, d), lambda i: (0, 0), pipeline_mode=pl.Buffered(1)),
                  pl.BlockSpec((1, HEAD_DIM), lambda i: (0, 0)),
                  pl.BlockSpec((1, d), lambda i: (0, 0)),
                  pl.BlockSpec((1, d), lambda i: (0, 0))],
        out_specs=[pl.BlockSpec((tm, d), lambda i: (i, 0)), pl.BlockSpec((tm, d), lambda i: (i, 0))],
        out_shape=[jax.ShapeDtypeStruct((m, d), F32), jax.ShapeDtypeStruct((m, d), BF16)],
        compiler_params=_params(1),
        name="cross_attention",
    )(h, k, v, wq, wo, row(gq), row(g_in), row(g_out))


def _layer(h, mem, l, p):
    b, s, d = h.shape
    m = b * s
    lam_init = 0.8 - 0.6 * math.exp(-0.3 * l)
    bf = lambda w: w.astype(BF16)
    h2d = h.reshape(m, d)

    proj = _in_projection(h2d, p["norm_mix_g"][l], p["w_in"][l], _head_gains(p, l), _rope_tables(s), s,
                          *IN_PROJ_TILE)
    proj = proj.reshape(b, s, IN_WIDTH)
    out_a = _moba_attention(proj, _score_bound(p["q_norm_a"][l], p["k_norm_a"][l]))
    out_b = _diff_attention(proj, _score_bound(p["q_norm_b"][l], p["k_norm_b"][l]),
                            p["lam_q1"][l], p["lam_k1"][l], p["lam_q2"][l], p["lam_k2"][l],
                            p["diff_subln_g"][l], lam_init)
    h2d = _out_projection(out_a.reshape(m, -1), out_b.reshape(m, -1), bf(p["w_out"][l]), h2d,
                          *OUT_PROJ_TILE)

    mem_len = mem.shape[1]
    mk, mv = _mem_kv(mem.reshape(b * mem_len, d), p["w_mk"][l], p["w_mv"][l], p["norm_mem_g"][l],
                     p["k_norm_m"][l], MEM_KV_TILE)
    h2d, hn = _cross_attention(h2d, mk.reshape(b, mem_len, -1), mv.reshape(b, mem_len, -1),
                               bf(p["w_mq"][l]), bf(p["w_mo"][l]), p["q_norm_m"][l],
                               p["norm_cross_g"][l], p["norm_ffn_g"][l], s, CROSS_TILE)

    act = _swiglu_up(hn, p["w_gate"][l], p["w_up"][l], *SWIGLU_TILE)
    h2d = _matmul_residual(act, bf(p["w_down"][l]), h2d, *DOWN_PROJ_TILE)
    return h2d.reshape(b, s, d)


def kernel(x, mem, norm_mix_g, w_in, q_norm_a, k_norm_a, q_norm_b, k_norm_b, lam_q1, lam_k1, lam_q2, lam_k2, diff_subln_g, w_out, norm_cross_g, norm_mem_g, w_mq, w_mk, w_mv, w_mo, q_norm_m, k_norm_m, norm_ffn_g, w_gate, w_up, w_down):
    p = dict(norm_mix_g=norm_mix_g, w_in=w_in, q_norm_a=q_norm_a, k_norm_a=k_norm_a, q_norm_b=q_norm_b,
             k_norm_b=k_norm_b, lam_q1=lam_q1, lam_k1=lam_k1, lam_q2=lam_q2, lam_k2=lam_k2,
             diff_subln_g=diff_subln_g, w_out=w_out, norm_cross_g=norm_cross_g, norm_mem_g=norm_mem_g,
             w_mq=w_mq, w_mk=w_mk, w_mv=w_mv, w_mo=w_mo, q_norm_m=q_norm_m, k_norm_m=k_norm_m,
             norm_ffn_g=norm_ffn_g, w_gate=w_gate, w_up=w_up, w_down=w_down)
    h = x
    for l in range(w_in.shape[0]):
        h = _layer(h, mem, l, p)
    return h
```

```python
import functools
import math

import jax
import jax.numpy as jnp
from jax import lax
from jax.experimental import pallas as pl
from jax.experimental.pallas import tpu as pltpu

F32 = jnp.float32
BF16 = jnp.bfloat16

HEAD_DIM = 128
MOBA_HEADS = 16
DIFF_HEADS = 8
MOBA_WIDTH = MOBA_HEADS * HEAD_DIM
DIFF_V_DIM = 2 * HEAD_DIM
IN_WIDTH = 12288
MOBA_BLOCK = 256
MOBA_TOPK = 3
MOBA_HEADS_PER_STEP = 2
ROPE_THETA = 500000.0
ROT_DIM = HEAD_DIM // 4
MEM_HEADS = 4
MEM_WIDTH = MEM_HEADS * HEAD_DIM
NORM_EPS = 1e-6
ATTN_SCALE = HEAD_DIM ** -0.5
MASKED = -1e30
EXP2_SCALE = ATTN_SCALE * math.log2(math.e)
SCORE_BOUND_MARGIN = 1.03
MAX_USABLE_SCORE_BOUND = 60.0

V7X_VMEM_BYTES = 64 * 1024 * 1024
VMEM_LIMIT = V7X_VMEM_BYTES - 8 * 1024 * 1024

NORM_CHUNK = 128
IN_PROJ_TILE = (1024, 512)
OUT_PROJ_TILE = (1024, 1024)
MEM_KV_TILE = 256
CROSS_TILE = 512
SWIGLU_TILE = (2048, 256)
DOWN_PROJ_TILE = (512, 512)

NT_DIMS = (((1,), (1,)), ((), ()))


def _params(n_grid):
    return pltpu.CompilerParams(
        dimension_semantics=("parallel",) * (n_grid - 1) + ("arbitrary",),
        vmem_limit_bytes=VMEM_LIMIT,
    )


def _row_rmsnorm(x, g):
    ms = jnp.mean(x * x, axis=-1, keepdims=True)
    return x * lax.rsqrt(ms + NORM_EPS) * g


def _mm_res_kernel(x_ref, w_ref, r_ref, o_ref):
    o_ref[...] = r_ref[...] + jnp.dot(x_ref[...], w_ref[...], preferred_element_type=F32)


def _matmul_residual(x, w, r, tm, tn):
    m, k = x.shape
    n = w.shape[1]
    return pl.pallas_call(
        _mm_res_kernel,
        grid=(m // tm, n // tn),
        in_specs=[
            pl.BlockSpec((tm, k), lambda i, j: (i, 0)),
            pl.BlockSpec((k, tn), lambda i, j: (0, j)),
            pl.BlockSpec((tm, tn), lambda i, j: (i, j)),
        ],
        out_specs=pl.BlockSpec((tm, tn), lambda i, j: (i, j)),
        out_shape=jax.ShapeDtypeStruct((m, n), F32),
        compiler_params=_params(2),
        name="matmul_residual",
    )(x, w, r)


def _mm2_res_kernel(a_ref, b_ref, wa_ref, wb_ref, r_ref, o_ref):
    acc = jnp.dot(a_ref[...], wa_ref[...].astype(BF16), preferred_element_type=F32)
    acc = acc + jnp.dot(b_ref[...], wb_ref[...].astype(BF16), preferred_element_type=F32)
    o_ref[...] = r_ref[...] + acc


def _out_projection(a, b, w, r, tm, tn):
    m, ka = a.shape
    kb = b.shape[1]
    assert ka == kb and w.shape[0] == ka + kb
    n = w.shape[1]
    return pl.pallas_call(
        _mm2_res_kernel,
        grid=(m // tm, n // tn),
        in_specs=[
            pl.BlockSpec((tm, ka), lambda i, j: (i, 0)),
            pl.BlockSpec((tm, kb), lambda i, j: (i, 0)),
            pl.BlockSpec((ka, tn), lambda i, j: (0, j)),
            pl.BlockSpec((kb, tn), lambda i, j: (1, j)),
            pl.BlockSpec((tm, tn), lambda i, j: (i, j)),
        ],
        out_specs=pl.BlockSpec((tm, tn), lambda i, j: (i, j)),
        out_shape=jax.ShapeDtypeStruct((m, n), F32),
        compiler_params=_params(2),
        name="out_projection",
    )(a, b, w, w, r)


def _swiglu_kernel(x_ref, wg_ref, wu_ref, o_ref):
    x = x_ref[...]
    g = jnp.dot(x, wg_ref[...].astype(BF16), preferred_element_type=F32)
    u = jnp.dot(x, wu_ref[...].astype(BF16), preferred_element_type=F32)
    o_ref[...] = (jax.nn.silu(g) * u).astype(o_ref.dtype)


def _swiglu_up(x, wg, wu, tm, tn):
    m, k = x.shape
    n = wg.shape[1]
    return pl.pallas_call(
        _swiglu_kernel,
        grid=(m // tm, n // tn),
        in_specs=[
            pl.BlockSpec((tm, k), lambda i, j: (i, 0)),
            pl.BlockSpec((k, tn), lambda i, j: (0, j)),
            pl.BlockSpec((k, tn), lambda i, j: (0, j)),
        ],
        out_specs=pl.BlockSpec((tm, tn), lambda i, j: (i, j)),
        out_shape=jax.ShapeDtypeStruct((m, n), BF16),
        compiler_params=_params(2),
        name="swiglu_up",
    )(x, wg, wu)


def _rope_tables(seq):
    half = ROT_DIM // 2
    pos = jnp.arange(seq, dtype=F32)
    inv = ROPE_THETA ** (-jnp.arange(0, ROT_DIM, 2, dtype=F32) / ROT_DIM)
    ang = pos[:, None] * inv[None, :]
    cos, sin = jnp.cos(ang), jnp.sin(ang)
    rest = HEAD_DIM - ROT_DIM
    c = jnp.concatenate([cos, cos, jnp.ones((seq, rest), F32)], axis=-1)
    sa = jnp.concatenate([-sin, jnp.zeros((seq, HEAD_DIM - half), F32)], axis=-1)
    sb = jnp.concatenate([jnp.zeros((seq, half), F32), sin, jnp.zeros((seq, rest), F32)], axis=-1)
    return jnp.stack([c, sa, sb])


def _head_gains(p, l):
    ones = jnp.ones((HEAD_DIM,), F32)
    rows = [p["q_norm_a"][l].astype(F32) * EXP2_SCALE, p["k_norm_a"][l].astype(F32), ones,
            p["q_norm_b"][l].astype(F32) * EXP2_SCALE, p["k_norm_b"][l].astype(F32), ones]
    return jnp.stack(rows).reshape(6, 1, HEAD_DIM)


def _norm_rope(x, g, c, sa, sb):
    half = ROT_DIM // 2
    ms = jnp.mean(x * x, axis=-1, keepdims=True)
    y = x * lax.rsqrt(ms + NORM_EPS) * g
    up = pltpu.roll(y, HEAD_DIM - half, 1)
    down = pltpu.roll(y, half, 1)
    return y * c + up * sa + down * sb


def _in_proj_kernel(nj, tiles, x_ref, gx_ref, w_ref, g_ref, rope_ref, o_ref, xn_ref, acc_ref):
    t = pl.program_id(0)
    tm, tn = o_ref.shape
    chunks = tm // NORM_CHUNK
    u = t - chunks
    tile = jnp.clip(u, 0, tiles - 1)
    row_tile, col_tile = tile // nj, tile % nj

    def normalise_chunk(slot, chunk):
        rows = pl.ds(pl.multiple_of(chunk * NORM_CHUNK, NORM_CHUNK), NORM_CHUNK)
        xn_ref[slot, rows, :] = _row_rmsnorm(x_ref[...], gx_ref[...]).astype(xn_ref.dtype)

    @pl.when(t == 0)
    def _():
        acc_ref[...] = jnp.zeros_like(acc_ref)

    @pl.when(u < 0)
    def _():
        normalise_chunk(0, t)

    def main_step(with_chunk):
        prev = acc_ref[...]
        acc_ref[...] = jnp.dot(xn_ref[row_tile % 2], w_ref[...].astype(BF16), preferred_element_type=F32)
        if with_chunk:
            normalise_chunk((row_tile + 1) % 2, col_tile - 1)
        return prev

    def finish_value(prev):
        o_ref[...] = prev.astype(o_ref.dtype)

    def finish_qk(prev):
        for h in range(tn // HEAD_DIM):
            cols = slice(h * HEAD_DIM, (h + 1) * HEAD_DIM)
            out = _norm_rope(prev[:, cols], g_ref[0], rope_ref[0], rope_ref[1], rope_ref[2])
            o_ref[:, cols] = out.astype(o_ref.dtype)

    region = ((jnp.clip(u - 1, 0, tiles - 1) % nj) * tn) // MOBA_WIDTH
    is_value = region % 3 == 2
    has_chunk = jnp.logical_and(col_tile >= 1, col_tile <= chunks)
    variants = ((is_value, finish_value, False),
                (jnp.logical_and(jnp.logical_not(is_value), has_chunk), finish_qk, True),
                (jnp.logical_and(jnp.logical_not(is_value), jnp.logical_not(has_chunk)), finish_qk, False))
    for cond, finish, with_chunk in variants:
        pl.when(jnp.logical_and(u >= 0, cond))(
            lambda finish=finish, with_chunk=with_chunk: finish(main_step(with_chunk)))


def _in_projection(x, gx, w, gains, tables, seq, tm, tn):
    m, k = x.shape
    n = w.shape[1]
    assert MOBA_WIDTH % tn == 0 and seq % tm == 0 and tm % NORM_CHUNK == 0
    nj = n // tn
    tiles = (m // tm) * nj
    chunks = tm // NORM_CHUNK
    assert chunks * tn <= 2 * MOBA_WIDTH
    mm = lambda t: jnp.clip(t - chunks, 0, tiles - 1)
    ep = lambda t: jnp.clip(t - chunks - 1, 0, tiles - 1)
    region = lambda t: ((ep(t) % nj) * tn) // MOBA_WIDTH

    def x_chunk(t):
        ahead = (mm(t) // nj + 1) * chunks + jnp.clip(mm(t) % nj - 1, 0, chunks - 1)
        return jnp.where(t < chunks, t, jnp.minimum(ahead, m // NORM_CHUNK - 1))

    return pl.pallas_call(
        functools.partial(_in_proj_kernel, nj, tiles),
        grid=(chunks + tiles + 1,),
        in_specs=[pl.BlockSpec((NORM_CHUNK, k), lambda t: (x_chunk(t), 0)),
                  pl.BlockSpec((1, k), lambda t: (0, 0)),
                  pl.BlockSpec((k, tn), lambda t: (0, mm(t) % nj)),
                  pl.BlockSpec((1, 1, HEAD_DIM), lambda t: (region(t), 0, 0)),
                  pl.BlockSpec((3, tm, HEAD_DIM), lambda t: (0, (ep(t) // nj) % (seq // tm), 0))],
        out_specs=pl.BlockSpec((tm, tn), lambda t: (ep(t) // nj, ep(t) % nj)),
        out_shape=jax.ShapeDtypeStruct((m, n), BF16),
        scratch_shapes=[pltpu.VMEM((2, tm, k), BF16),
                        pltpu.VMEM((tm, tn), F32)],
        compiler_params=_params(1),
        name="in_projection",
    )(x, gx.reshape(1, k).astype(F32), w, gains, tables)


def _causal_mask(s):
    key = lax.broadcasted_iota(jnp.int32, s.shape, 0)
    qry = lax.broadcasted_iota(jnp.int32, s.shape, 1)
    return jnp.where(key <= qry, s, MASKED)


def _prefix_softmax(blocks, shift):
    if shift is None:
        m = jnp.max(blocks[0], axis=0, keepdims=True)
        for sj in blocks[1:]:
            m = jnp.maximum(m, jnp.max(sj, axis=0, keepdims=True))
    else:
        m = shift
    ps, l = [], None
    for sj in blocks:
        pj = jnp.exp2(sj - m)
        lj = jnp.sum(pj, axis=0, keepdims=True)
        l = lj if l is None else l + lj
        ps.append(pj)
    return ps, l


def _score_bound(gq, gk):
    bound = (HEAD_DIM * SCORE_BOUND_MARGIN * jnp.max(jnp.abs(gq.astype(F32))) * EXP2_SCALE
             * jnp.max(jnp.abs(gk.astype(F32))))
    usable = (bound <= MAX_USABLE_SCORE_BOUND).astype(F32)
    return jnp.stack([bound, usable])


def _run_with_score_bound(bound_ref, run):
    usable = bound_ref[1] > 0.5

    @pl.when(usable)
    def _():
        run(bound_ref[0])

    @pl.when(jnp.logical_not(usable))
    def _():
        run(None)


def _split_blocks(s, n):
    return [s[j * MOBA_BLOCK:(j + 1) * MOBA_BLOCK, :] for j in range(n)]


def _moba_kernel(bound_ref, q_ref, k_ref, v_ref, o_ref, vt_ref, kmean_ref):
    blk = MOBA_BLOCK
    nb = q_ref.shape[1] // blk
    heads = q_ref.shape[2] // HEAD_DIM
    lanes = lambda h: slice(h * HEAD_DIM, (h + 1) * HEAD_DIM)

    def masked_scores(h, qi):
        rows = slice(qi * blk, (qi + 1) * blk)
        nk = (qi + 1) * blk
        qb = q_ref[0, rows, lanes(h)]
        s = lax.dot_general(k_ref[0, 0:nk, lanes(h)], qb, NT_DIMS, preferred_element_type=F32)
        blocks = _split_blocks(s, qi + 1)
        blocks[qi] = _causal_mask(blocks[qi])

        if qi > MOBA_TOPK:
            gate = lax.dot_general(kmean_ref[h], qb.astype(F32), NT_DIMS, preferred_element_type=F32,
                                   precision=lax.Precision.HIGHEST)
            n_iota = lax.broadcasted_iota(jnp.int32, gate.shape, 0)
            beaten = jnp.zeros(gate.shape, jnp.int32)
            for mth in range(qi):
                gm = gate[mth:mth + 1, :]
                wins = (gm > gate) | ((gm == gate) & (mth < n_iota))
                beaten = beaten + wins.astype(jnp.int32)
            bias = jnp.where(beaten < MOBA_TOPK, 0.0, MASKED)
            for j in range(qi):
                blocks[j] = blocks[j] + bias[j:j + 1, :]
        return blocks

    def softmax(blocks, shift):
        ps, l = _prefix_softmax(blocks, shift)
        return jnp.concatenate([pj.astype(BF16) for pj in ps], axis=0), l

    def weighted_values(h, qi, p, l):
        rows = slice(qi * blk, (qi + 1) * blk)
        acc = jnp.dot(vt_ref[h, :, 0:(qi + 1) * blk], p, preferred_element_type=F32)
        o_ref[0, rows, lanes(h)] = (acc / l).T.astype(o_ref.dtype)

    def run(shift):
        for h in range(heads):
            for n in range(nb):
                rows = slice(n * blk, (n + 1) * blk)
                kmean_ref[h, n:n + 1, :] = jnp.mean(k_ref[0, rows, lanes(h)].astype(F32), axis=0, keepdims=True)
                vt_ref[h, :, rows] = v_ref[0, rows, lanes(h)].astype(F32).T.astype(BF16)

        items = [(h, qi) for qi in range(nb) for h in range(heads)]
        ahead = masked_scores(*items[0])
        for idx, item in enumerate(items):
            blocks = ahead
            if idx + 1 < len(items):
                ahead = masked_scores(*items[idx + 1])
            weighted_values(*item, *softmax(blocks, shift))

    _run_with_score_bound(bound_ref, run)


def _moba_attention(proj, bound):
    b, s, _ = proj.shape
    nb = s // MOBA_BLOCK
    hp = MOBA_HEADS_PER_STEP
    width = hp * HEAD_DIM
    head = lambda off: pl.BlockSpec((1, s, width), lambda bi, h: (bi, 0, off + h))
    return pl.pallas_call(
        _moba_kernel,
        grid=(b, MOBA_HEADS // hp),
        in_specs=[pl.BlockSpec(memory_space=pltpu.SMEM),
                  head(0), head(MOBA_HEADS // hp), head(2 * MOBA_HEADS // hp)],
        out_specs=pl.BlockSpec((1, s, width), lambda bi, h: (bi, 0, h)),
        out_shape=jax.ShapeDtypeStruct((b, s, MOBA_WIDTH), BF16),
        scratch_shapes=[
            pltpu.VMEM((hp, HEAD_DIM, s), BF16),
            pltpu.VMEM((hp, nb, HEAD_DIM), F32),
        ],
        compiler_params=_params(2),
        name="moba_attention",
    )(bound, proj, proj, proj)


def _diff_kernel(lam_init, bound_ref, q_ref, k_ref, v_ref, lq1_ref, lk1_ref, lq2_ref, lk2_ref, gs_ref,
                 o_ref, vt_ref):
    blk = MOBA_BLOCK
    nb = q_ref.shape[1] // blk
    hd = HEAD_DIM

    lam = (jnp.exp(jnp.sum(lq1_ref[...] * lk1_ref[...], axis=-1, keepdims=True))
           - jnp.exp(jnp.sum(lq2_ref[...] * lk2_ref[...], axis=-1, keepdims=True))
           + lam_init)

    def masked_scores(qi):
        rows = slice(qi * blk, (qi + 1) * blk)
        nk = (qi + 1) * blk
        out = []
        for half in range(2):
            cols = slice(half * hd, (half + 1) * hd)
            s = lax.dot_general(k_ref[0, 0:nk, cols], q_ref[0, rows, cols], NT_DIMS,
                                preferred_element_type=F32)
            blocks = _split_blocks(s, qi + 1)
            blocks[qi] = _causal_mask(blocks[qi])
            out.append(blocks)
        return out

    def attend(qi, blocks1, blocks2, shift):
        rows = slice(qi * blk, (qi + 1) * blk)
        nk = (qi + 1) * blk
        p1, l1 = _prefix_softmax(blocks1, shift)
        p2, l2 = _prefix_softmax(blocks2, shift)
        r1 = 1.0 / l1
        r2 = lam / l2
        w = jnp.concatenate([(a * r1 - b * r2).astype(BF16) for a, b in zip(p1, p2)], axis=0)
        o = jnp.dot(vt_ref[:, 0:nk], w, preferred_element_type=F32)
        ms = jnp.mean(o * o, axis=0, keepdims=True)
        on = (o * lax.rsqrt(ms + NORM_EPS)).T * gs_ref[...]
        o_ref[0, rows, :] = (on * (1.0 - lam_init)).astype(o_ref.dtype)

    def run(shift):
        for n in range(nb):
            rows = slice(n * blk, (n + 1) * blk)
            vt_ref[:, rows] = v_ref[0, rows, :].astype(F32).T.astype(BF16)

        ahead = masked_scores(0)
        for qi in range(nb):
            blocks = ahead
            if qi + 1 < nb:
                ahead = masked_scores(qi + 1)
            attend(qi, *blocks, shift)

    _run_with_score_bound(bound_ref, run)


def _diff_attention(proj, bound, lq1, lk1, lq2, lk2, gs, lam_init):
    b, s, _ = proj.shape
    qoff = 3 * MOBA_WIDTH // DIFF_V_DIM
    head = lambda off: pl.BlockSpec((1, s, DIFF_V_DIM), lambda bi, h: (bi, 0, off + h))
    full = lambda shape: pl.BlockSpec(shape, lambda bi, h: (0,) * len(shape))
    row = lambda v: v.reshape(1, -1).astype(F32)
    return pl.pallas_call(
        functools.partial(_diff_kernel, lam_init),
        grid=(b, DIFF_HEADS),
        in_specs=[pl.BlockSpec(memory_space=pltpu.SMEM),
                  head(qoff), head(qoff + DIFF_HEADS), head(qoff + 2 * DIFF_HEADS),
                  full((1, HEAD_DIM)), full((1, HEAD_DIM)), full((1, HEAD_DIM)), full((1, HEAD_DIM)),
                  full((1, DIFF_V_DIM))],
        out_specs=pl.BlockSpec((1, s, DIFF_V_DIM), lambda bi, h: (bi, 0, h)),
        out_shape=jax.ShapeDtypeStruct((b, s, DIFF_HEADS * DIFF_V_DIM), BF16),
        scratch_shapes=[pltpu.VMEM((DIFF_V_DIM, s), BF16)],
        compiler_params=_params(2),
        name="diff_attention",
    )(bound, proj, proj, proj, row(lq1), row(lk1), row(lq2), row(lk2), row(gs))


def _mem_kv_kernel(m_ref, wk_ref, wv_ref, gm_ref, gk_ref, k_ref, v_ref):
    x = _row_rmsnorm(m_ref[...], gm_ref[...]).astype(BF16)
    k = jnp.dot(x, wk_ref[...].astype(BF16), preferred_element_type=F32)
    for h in range(MEM_HEADS):
        cols = slice(h * HEAD_DIM, (h + 1) * HEAD_DIM)
        k_ref[:, cols] = _row_rmsnorm(k[:, cols], gk_ref[...]).astype(k_ref.dtype)
    v_ref[...] = jnp.dot(x, wv_ref[...].astype(BF16), preferred_element_type=F32).astype(v_ref.dtype)


def _mem_kv(mem, wk, wv, g_mem, gk, tm):
    m, d = mem.shape
    row = lambda g: g.reshape(1, -1).astype(F32)
    return pl.pallas_call(
        _mem_kv_kernel,
        grid=(m // tm,),
        in_specs=[pl.BlockSpec((tm, d), lambda i: (i, 0)),
                  pl.BlockSpec((d, MEM_WIDTH), lambda i: (0, 0)),
                  pl.BlockSpec((d, MEM_WIDTH), lambda i: (0, 0)),
                  pl.BlockSpec((1, d), lambda i: (0, 0)),
                  pl.BlockSpec((1, HEAD_DIM), lambda i: (0, 0))],
        out_specs=[pl.BlockSpec((tm, MEM_WIDTH), lambda i: (i, 0)),
                   pl.BlockSpec((tm, MEM_WIDTH), lambda i: (i, 0))],
        out_shape=[jax.ShapeDtypeStruct((m, MEM_WIDTH), BF16), jax.ShapeDtypeStruct((m, MEM_WIDTH), BF16)],
        compiler_params=_params(1),
        name="mem_kv",
    )(mem, wk, wv, row(g_mem), row(gk))


def _cross_kernel(h_ref, k_ref, v_ref, wq_ref, wo_ref, gq_ref, gin_ref, gout_ref, o_ref, on_ref):
    hn = _row_rmsnorm(h_ref[...], gin_ref[...]).astype(BF16)
    q = jnp.dot(hn, wq_ref[...], preferred_element_type=F32)
    heads = []
    for h in range(MEM_HEADS):
        cols = slice(h * HEAD_DIM, (h + 1) * HEAD_DIM)
        qh = _row_rmsnorm(q[:, cols], gq_ref[...]).astype(BF16)
        s = lax.dot_general(qh, k_ref[0, :, cols], NT_DIMS, preferred_element_type=F32) * ATTN_SCALE
        e = jnp.exp(s - jnp.max(s, axis=-1, keepdims=True))
        p = (e / jnp.sum(e, axis=-1, keepdims=True)).astype(BF16)
        heads.append(jnp.dot(p, v_ref[0, :, cols], preferred_element_type=F32).astype(BF16))
    o = jnp.concatenate(heads, axis=-1)
    out = h_ref[...] + jnp.dot(o, wo_ref[...], preferred_element_type=F32)
    o_ref[...] = out
    on_ref[...] = _row_rmsnorm(out, gout_ref[...]).astype(on_ref.dtype)


def _cross_attention(h, k, v, wq, wo, gq, g_in, g_out, seq, tm):
    m, d = h.shape
    mem_len = k.shape[1]
    per_batch = seq // tm
    row = lambda g: g.reshape(1, -1).astype(F32)
    return pl.pallas_call(
        _cross_kernel,
        grid=(m // tm,),
        in_specs=[pl.BlockSpec((tm, d), lambda i: (i, 0)),
                  pl.BlockSpec((1, mem_len, MEM_WIDTH), lambda i: (i // per_batch, 0, 0)),
                  pl.BlockSpec((1, mem_len, MEM_WIDTH), lambda i: (i // per_batch, 0, 0)),
                  pl.BlockSpec((d, MEM_WIDTH), lambda i: (0, 0), pipeline_mode=pl.Buffered(1)),
                  pl.BlockSpec((MEM_WIDTH, d), lambda i: (0, 0), pipeline_mode=pl.Buffered(1)),
                  pl.BlockSpec((1, HEAD_DIM), lambda i: (0, 0)),
                  pl.BlockSpec((1, d), lambda i: (0, 0)),
                  pl.BlockSpec((1, d), lambda i: (0, 0))],
        out_specs=[pl.BlockSpec((tm, d), lambda i: (i, 0)), pl.BlockSpec((tm, d), lambda i: (i, 0))],
        out_shape=[jax.ShapeDtypeStruct((m, d), F32), jax.ShapeDtypeStruct((m, d), BF16)],
        compiler_params=_params(1),
        name="cross_attention",
    )(h, k, v, wq, wo, row(gq), row(g_in), row(g_out))


def _layer(h, mem, l, p):
    b, s, d = h.shape
    m = b * s
    lam_init = 0.8 - 0.6 * math.exp(-0.3 * l)
    bf = lambda w: w.astype(BF16)
    h2d = h.reshape(m, d)

    proj = _in_projection(h2d, p["norm_mix_g"][l], p["w_in"][l], _head_gains(p, l), _rope_tables(s), s,
                          *IN_PROJ_TILE)
    proj = proj.reshape(b, s, IN_WIDTH)
    out_a = _moba_attention(proj, _score_bound(p["q_norm_a"][l], p["k_norm_a"][l]))
    out_b = _diff_attention(proj, _score_bound(p["q_norm_b"][l], p["k_norm_b"][l]),
                            p["lam_q1"][l], p["lam_k1"][l], p["lam_q2"][l], p["lam_k2"][l],
                            p["diff_subln_g"][l], lam_init)
    h2d = _out_projection(out_a.reshape(m, -1), out_b.reshape(m, -1), bf(p["w_out"][l]), h2d,
                          *OUT_PROJ_TILE)

    mem_len = mem.shape[1]
    mk, mv = _mem_kv(mem.reshape(b * mem_len, d), p["w_mk"][l], p["w_mv"][l], p["norm_mem_g"][l],
                     p["k_norm_m"][l], MEM_KV_TILE)
    h2d, hn = _cross_attention(h2d, mk.reshape(b, mem_len, -1), mv.reshape(b, mem_len, -1),
                               bf(p["w_mq"][l]), bf(p["w_mo"][l]), p["q_norm_m"][l],
                               p["norm_cross_g"][l], p["norm_ffn_g"][l], s, CROSS_TILE)

    act = _swiglu_up(hn, p["w_gate"][l], p["w_up"][l], *SWIGLU_TILE)
    h2d = _matmul_residual(act, bf(p["w_down"][l]), h2d, *DOWN_PROJ_TILE)
    return h2d.reshape(b, s, d)


def kernel(x, mem, norm_mix_g, w_in, q_norm_a, k_norm_a, q_norm_b, k_norm_b, lam_q1, lam_k1, lam_q2, lam_k2, diff_subln_g, w_out, norm_cross_g, norm_mem_g, w_mq, w_mk, w_mv, w_mo, q_norm_m, k_norm_m, norm_ffn_g, w_gate, w_up, w_down):
    p = dict(norm_mix_g=norm_mix_g, w_in=w_in, q_norm_a=q_norm_a, k_norm_a=k_norm_a, q_norm_b=q_norm_b,
             k_norm_b=k_norm_b, lam_q1=lam_q1, lam_k1=lam_k1, lam_q2=lam_q2, lam_k2=lam_k2,
             diff_subln_g=diff_subln_g, w_out=w_out, norm_cross_g=norm_cross_g, norm_mem_g=norm_mem_g,
             w_mq=w_mq, w_mk=w_mk, w_mv=w_mv, w_mo=w_mo, q_norm_m=q_norm_m, k_norm_m=k_norm_m,
             norm_ffn_g=norm_ffn_g, w_gate=w_gate, w_up=w_up, w_down=w_down)
    h = x
    for l in range(w_in.shape[0]):
        h = _layer(h, mem, l, p)
    return h
```

```python
import functools
import math

import jax
import jax.numpy as jnp
from jax import lax
from jax.experimental import pallas as pl
from jax.experimental.pallas import tpu as pltpu

F32 = jnp.float32
BF16 = jnp.bfloat16

HEAD_DIM = 128
MOBA_HEADS = 16
DIFF_HEADS = 8
MOBA_WIDTH = MOBA_HEADS * HEAD_DIM
DIFF_V_DIM = 2 * HEAD_DIM
IN_WIDTH = 12288
MOBA_BLOCK = 256
MOBA_TOPK = 3
MOBA_HEADS_PER_STEP = 2
ROPE_THETA = 500000.0
ROT_DIM = HEAD_DIM // 4
MEM_HEADS = 4
MEM_WIDTH = MEM_HEADS * HEAD_DIM
NORM_EPS = 1e-6
ATTN_SCALE = HEAD_DIM ** -0.5
MASKED = -1e30
EXP2_SCALE = ATTN_SCALE * math.log2(math.e)
SCORE_BOUND_MARGIN = 1.03
MAX_USABLE_SCORE_BOUND = 60.0

V7X_VMEM_BYTES = 64 * 1024 * 1024
VMEM_LIMIT = V7X_VMEM_BYTES - 8 * 1024 * 1024

NORM_CHUNK = 128
IN_PROJ_TILE = (1024, 512)
OUT_PROJ_TILE = (1024, 1024)
MEM_KV_TILE = 256
CROSS_TILE = 512
SWIGLU_TILE = (2048, 256)
DOWN_PROJ_TILE = (512, 512)

BF16_SUBLANES = 16
NT_DIMS = (((1,), (1,)), ((), ()))


def _params(n_grid):
    return pltpu.CompilerParams(
        dimension_semantics=("parallel",) * (n_grid - 1) + ("arbitrary",),
        vmem_limit_bytes=VMEM_LIMIT,
    )


def _row_rmsnorm(x, g):
    ms = jnp.mean(x * x, axis=-1, keepdims=True)
    return x * lax.rsqrt(ms + NORM_EPS) * g


def _mm_res_kernel(x_ref, w_ref, r_ref, o_ref):
    o_ref[...] = r_ref[...] + jnp.dot(x_ref[...], w_ref[...], preferred_element_type=F32)


def _matmul_residual(x, w, r, tm, tn):
    m, k = x.shape
    n = w.shape[1]
    return pl.pallas_call(
        _mm_res_kernel,
        grid=(m // tm, n // tn),
        in_specs=[
            pl.BlockSpec((tm, k), lambda i, j: (i, 0)),
            pl.BlockSpec((k, tn), lambda i, j: (0, j)),
            pl.BlockSpec((tm, tn), lambda i, j: (i, j)),
        ],
        out_specs=pl.BlockSpec((tm, tn), lambda i, j: (i, j)),
        out_shape=jax.ShapeDtypeStruct((m, n), F32),
        compiler_params=_params(2),
        name="matmul_residual",
    )(x, w, r)


def _mm2_res_kernel(a_ref, b_ref, wa_ref, wb_ref, r_ref, o_ref):
    acc = jnp.dot(a_ref[...], wa_ref[...].astype(BF16), preferred_element_type=F32)
    acc = acc + jnp.dot(b_ref[...], wb_ref[...].astype(BF16), preferred_element_type=F32)
    o_ref[...] = r_ref[...] + acc


def _out_projection(a, b, w, r, tm, tn):
    m, ka = a.shape
    kb = b.shape[1]
    assert ka == kb and w.shape[0] == ka + kb
    n = w.shape[1]
    return pl.pallas_call(
        _mm2_res_kernel,
        grid=(m // tm, n // tn),
        in_specs=[
            pl.BlockSpec((tm, ka), lambda i, j: (i, 0)),
            pl.BlockSpec((tm, kb), lambda i, j: (i, 0)),
            pl.BlockSpec((ka, tn), lambda i, j: (0, j)),
            pl.BlockSpec((kb, tn), lambda i, j: (1, j)),
            pl.BlockSpec((tm, tn), lambda i, j: (i, j)),
        ],
        out_specs=pl.BlockSpec((tm, tn), lambda i, j: (i, j)),
        out_shape=jax.ShapeDtypeStruct((m, n), F32),
        compiler_params=_params(2),
        name="out_projection",
    )(a, b, w, w, r)


def _swiglu_kernel(x_ref, wg_ref, wu_ref, wnext_ref, o_ref, wnext_bf_ref):
    x = x_ref[...]
    g = jnp.dot(x, wg_ref[...].astype(BF16), preferred_element_type=F32)
    u = jnp.dot(x, wu_ref[...].astype(BF16), preferred_element_type=F32)
    o_ref[...] = (jax.nn.silu(g) * u).astype(o_ref.dtype)
    wnext_bf_ref[...] = wnext_ref[...].astype(BF16)


def _swiglu_up(x, wg, wu, w_next, tm, tn):
    m, k = x.shape
    n = wg.shape[1]
    ni, nj = m // tm, n // tn
    slab = _slab_rows(w_next.shape[0], ni * nj)
    slab_spec = pl.BlockSpec((slab, w_next.shape[1]),
                             lambda i, j: (jnp.minimum(i * nj + j, w_next.shape[0] // slab - 1), 0))
    return pl.pallas_call(
        _swiglu_kernel,
        grid=(ni, nj),
        in_specs=[
            pl.BlockSpec((tm, k), lambda i, j: (i, 0)),
            pl.BlockSpec((k, tn), lambda i, j: (0, j)),
            pl.BlockSpec((k, tn), lambda i, j: (0, j)),
            slab_spec,
        ],
        out_specs=[pl.BlockSpec((tm, tn), lambda i, j: (i, j)), slab_spec],
        out_shape=[jax.ShapeDtypeStruct((m, n), BF16), jax.ShapeDtypeStruct(w_next.shape, BF16)],
        compiler_params=_params(2),
        name="swiglu_up",
    )(x, wg, wu, w_next)


def _slab_rows(rows, steps):
    slab = BF16_SUBLANES * (-(-rows // (BF16_SUBLANES * steps)))
    assert rows % slab == 0 and rows // slab <= steps
    return slab


def _rope_tables(seq):
    half = ROT_DIM // 2
    pos = jnp.arange(seq, dtype=F32)
    inv = ROPE_THETA ** (-jnp.arange(0, ROT_DIM, 2, dtype=F32) / ROT_DIM)
    ang = pos[:, None] * inv[None, :]
    cos, sin = jnp.cos(ang), jnp.sin(ang)
    rest = HEAD_DIM - ROT_DIM
    c = jnp.concatenate([cos, cos, jnp.ones((seq, rest), F32)], axis=-1)
    sa = jnp.concatenate([-sin, jnp.zeros((seq, HEAD_DIM - half), F32)], axis=-1)
    sb = jnp.concatenate([jnp.zeros((seq, half), F32), sin, jnp.zeros((seq, rest), F32)], axis=-1)
    return jnp.stack([c, sa, sb])


def _head_gains(p, l):
    ones = jnp.ones((HEAD_DIM,), F32)
    rows = [p["q_norm_a"][l].astype(F32) * EXP2_SCALE, p["k_norm_a"][l].astype(F32), ones,
            p["q_norm_b"][l].astype(F32) * EXP2_SCALE, p["k_norm_b"][l].astype(F32), ones]
    return jnp.stack(rows).reshape(6, 1, HEAD_DIM)


def _norm_rope(x, g, c, sa, sb):
    half = ROT_DIM // 2
    ms = jnp.mean(x * x, axis=-1, keepdims=True)
    y = x * lax.rsqrt(ms + NORM_EPS) * g
    up = pltpu.roll(y, HEAD_DIM - half, 1)
    down = pltpu.roll(y, half, 1)
    return y * c + up * sa + down * sb


def _in_proj_kernel(nj, tiles, x_ref, gx_ref, w_ref, g_ref, rope_ref, wnext_ref, o_ref, wnext_bf_ref,
                    xn_ref, acc_ref):
    t = pl.program_id(0)
    tm, tn = o_ref.shape
    wnext_bf_ref[...] = wnext_ref[...].astype(BF16)
    chunks = tm // NORM_CHUNK
    u = t - chunks
    tile = jnp.clip(u, 0, tiles - 1)
    row_tile, col_tile = tile // nj, tile % nj

    def normalise_chunk(slot, chunk):
        rows = pl.ds(pl.multiple_of(chunk * NORM_CHUNK, NORM_CHUNK), NORM_CHUNK)
        xn_ref[slot, rows, :] = _row_rmsnorm(x_ref[...], gx_ref[...]).astype(xn_ref.dtype)

    @pl.when(t == 0)
    def _():
        acc_ref[...] = jnp.zeros_like(acc_ref)

    @pl.when(u < 0)
    def _():
        normalise_chunk(0, t)

    def main_step(with_chunk):
        prev = acc_ref[...]
        acc_ref[...] = jnp.dot(xn_ref[row_tile % 2], w_ref[...].astype(BF16), preferred_element_type=F32)
        if with_chunk:
            normalise_chunk((row_tile + 1) % 2, col_tile - 1)
        return prev

    def finish_value(prev):
        o_ref[...] = prev.astype(o_ref.dtype)

    def finish_qk(prev):
        for h in range(tn // HEAD_DIM):
            cols = slice(h * HEAD_DIM, (h + 1) * HEAD_DIM)
            out = _norm_rope(prev[:, cols], g_ref[0], rope_ref[0], rope_ref[1], rope_ref[2])
            o_ref[:, cols] = out.astype(o_ref.dtype)

    region = ((jnp.clip(u - 1, 0, tiles - 1) % nj) * tn) // MOBA_WIDTH
    is_value = region % 3 == 2
    has_chunk = jnp.logical_and(col_tile >= 1, col_tile <= chunks)
    variants = ((is_value, finish_value, False),
                (jnp.logical_and(jnp.logical_not(is_value), has_chunk), finish_qk, True),
                (jnp.logical_and(jnp.logical_not(is_value), jnp.logical_not(has_chunk)), finish_qk, False))
    for cond, finish, with_chunk in variants:
        pl.when(jnp.logical_and(u >= 0, cond))(
            lambda finish=finish, with_chunk=with_chunk: finish(main_step(with_chunk)))


def _in_projection(x, gx, w, gains, tables, w_next, seq, tm, tn):
    m, k = x.shape
    n = w.shape[1]
    assert MOBA_WIDTH % tn == 0 and seq % tm == 0 and tm % NORM_CHUNK == 0
    nj = n // tn
    tiles = (m // tm) * nj
    chunks = tm // NORM_CHUNK
    assert chunks * tn <= 2 * MOBA_WIDTH
    mm = lambda t: jnp.clip(t - chunks, 0, tiles - 1)
    ep = lambda t: jnp.clip(t - chunks - 1, 0, tiles - 1)
    region = lambda t: ((ep(t) % nj) * tn) // MOBA_WIDTH

    slab = _slab_rows(w_next.shape[0], tiles)
    slab_spec = pl.BlockSpec((slab, w_next.shape[1]),
                             lambda t: (jnp.minimum(mm(t), w_next.shape[0] // slab - 1), 0))

    def x_chunk(t):
        ahead = (mm(t) // nj + 1) * chunks + jnp.clip(mm(t) % nj - 1, 0, chunks - 1)
        return jnp.where(t < chunks, t, jnp.minimum(ahead, m // NORM_CHUNK - 1))

    return pl.pallas_call(
        functools.partial(_in_proj_kernel, nj, tiles),
        grid=(chunks + tiles + 1,),
        in_specs=[pl.BlockSpec((NORM_CHUNK, k), lambda t: (x_chunk(t), 0)),
                  pl.BlockSpec((1, k), lambda t: (0, 0)),
                  pl.BlockSpec((k, tn), lambda t: (0, mm(t) % nj)),
                  pl.BlockSpec((1, 1, HEAD_DIM), lambda t: (region(t), 0, 0)),
                  pl.BlockSpec((3, tm, HEAD_DIM), lambda t: (0, (ep(t) // nj) % (seq // tm), 0)),
                  slab_spec],
        out_specs=[pl.BlockSpec((tm, tn), lambda t: (ep(t) // nj, ep(t) % nj)), slab_spec],
        out_shape=[jax.ShapeDtypeStruct((m, n), BF16), jax.ShapeDtypeStruct(w_next.shape, BF16)],
        scratch_shapes=[pltpu.VMEM((2, tm, k), BF16),
                        pltpu.VMEM((tm, tn), F32)],
        compiler_params=_params(1),
        name="in_projection",
    )(x, gx.reshape(1, k).astype(F32), w, gains, tables, w_next)


def _causal_mask(s):
    key = lax.broadcasted_iota(jnp.int32, s.shape, 0)
    qry = lax.broadcasted_iota(jnp.int32, s.shape, 1)
    return jnp.where(key <= qry, s, MASKED)


def _prefix_softmax(blocks, shift):
    if shift is None:
        m = jnp.max(blocks[0], axis=0, keepdims=True)
        for sj in blocks[1:]:
            m = jnp.maximum(m, jnp.max(sj, axis=0, keepdims=True))
    else:
        m = shift
    ps, l = [], None
    for sj in blocks:
        pj = jnp.exp2(sj - m)
        lj = jnp.sum(pj, axis=0, keepdims=True)
        l = lj if l is None else l + lj
        ps.append(pj)
    return ps, l


def _score_bound(gq, gk):
    bound = (HEAD_DIM * SCORE_BOUND_MARGIN * jnp.max(jnp.abs(gq.astype(F32))) * EXP2_SCALE
             * jnp.max(jnp.abs(gk.astype(F32))))
    usable = (bound <= MAX_USABLE_SCORE_BOUND).astype(F32)
    return jnp.stack([bound, usable])


def _run_with_score_bound(bound_ref, run):
    usable = bound_ref[1] > 0.5

    @pl.when(usable)
    def _():
        run(bound_ref[0])

    @pl.when(jnp.logical_not(usable))
    def _():
        run(None)


def _split_blocks(s, n):
    return [s[j * MOBA_BLOCK:(j + 1) * MOBA_BLOCK, :] for j in range(n)]


def _moba_kernel(bound_ref, q_ref, k_ref, v_ref, o_ref, vt_ref, kmean_ref):
    blk = MOBA_BLOCK
    nb = q_ref.shape[1] // blk
    heads = q_ref.shape[2] // HEAD_DIM
    lanes = lambda h: slice(h * HEAD_DIM, (h + 1) * HEAD_DIM)

    def masked_scores(h, qi):
        rows = slice(qi * blk, (qi + 1) * blk)
        nk = (qi + 1) * blk
        qb = q_ref[0, rows, lanes(h)]
        s = lax.dot_general(k_ref[0, 0:nk, lanes(h)], qb, NT_DIMS, preferred_element_type=F32)
        blocks = _split_blocks(s, qi + 1)
        blocks[qi] = _causal_mask(blocks[qi])

        if qi > MOBA_TOPK:
            gate = lax.dot_general(kmean_ref[h], qb.astype(F32), NT_DIMS, preferred_element_type=F32,
                                   precision=lax.Precision.HIGHEST)
            n_iota = lax.broadcasted_iota(jnp.int32, gate.shape, 0)
            beaten = jnp.zeros(gate.shape, jnp.int32)
            for mth in range(qi):
                gm = gate[mth:mth + 1, :]
                wins = (gm > gate) | ((gm == gate) & (mth < n_iota))
                beaten = beaten + wins.astype(jnp.int32)
            bias = jnp.where(beaten < MOBA_TOPK, 0.0, MASKED)
            for j in range(qi):
                blocks[j] = blocks[j] + bias[j:j + 1, :]
        return blocks

    def softmax(blocks, shift):
        ps, l = _prefix_softmax(blocks, shift)
        return jnp.concatenate([pj.astype(BF16) for pj in ps], axis=0), l

    def weighted_values(h, qi, p, l):
        rows = slice(qi * blk, (qi + 1) * blk)
        acc = jnp.dot(vt_ref[h, :, 0:(qi + 1) * blk], p, preferred_element_type=F32)
        o_ref[0, rows, lanes(h)] = (acc / l).T.astype(o_ref.dtype)

    def run(shift):
        for h in range(heads):
            for n in range(nb):
                rows = slice(n * blk, (n + 1) * blk)
                kmean_ref[h, n:n + 1, :] = jnp.mean(k_ref[0, rows, lanes(h)].astype(F32), axis=0, keepdims=True)
                vt_ref[h, :, rows] = v_ref[0, rows, lanes(h)].astype(F32).T.astype(BF16)

        items = [(h, qi) for qi in range(nb) for h in range(heads)]
        ahead = masked_scores(*items[0])
        for idx, item in enumerate(items):
            blocks = ahead
            if idx + 1 < len(items):
                ahead = masked_scores(*items[idx + 1])
            weighted_values(*item, *softmax(blocks, shift))

    _run_with_score_bound(bound_ref, run)


def _moba_attention(proj, bound):
    b, s, _ = proj.shape
    nb = s // MOBA_BLOCK
    hp = MOBA_HEADS_PER_STEP
    width = hp * HEAD_DIM
    head = lambda off: pl.BlockSpec((1, s, width), lambda bi, h: (bi, 0, off + h))
    return pl.pallas_call(
        _moba_kernel,
        grid=(b, MOBA_HEADS // hp),
        in_specs=[pl.BlockSpec(memory_space=pltpu.SMEM),
                  head(0), head(MOBA_HEADS // hp), head(2 * MOBA_HEADS // hp)],
        out_specs=pl.BlockSpec((1, s, width), lambda bi, h: (bi, 0, h)),
        out_shape=jax.ShapeDtypeStruct((b, s, MOBA_WIDTH), BF16),
        scratch_shapes=[
            pltpu.VMEM((hp, HEAD_DIM, s), BF16),
            pltpu.VMEM((hp, nb, HEAD_DIM), F32),
        ],
        compiler_params=_params(2),
        name="moba_attention",
    )(bound, proj, proj, proj)


def _diff_kernel(lam_init, bound_ref, q_ref, k_ref, v_ref, lq1_ref, lk1_ref, lq2_ref, lk2_ref, gs_ref,
                 o_ref, vt_ref):
    blk = MOBA_BLOCK
    nb = q_ref.shape[1] // blk
    hd = HEAD_DIM

    lam = (jnp.exp(jnp.sum(lq1_ref[...] * lk1_ref[...], axis=-1, keepdims=True))
           - jnp.exp(jnp.sum(lq2_ref[...] * lk2_ref[...], axis=-1, keepdims=True))
           + lam_init)

    def masked_scores(qi):
        rows = slice(qi * blk, (qi + 1) * blk)
        nk = (qi + 1) * blk
        out = []
        for half in range(2):
            cols = slice(half * hd, (half + 1) * hd)
            s = lax.dot_general(k_ref[0, 0:nk, cols], q_ref[0, rows, cols], NT_DIMS,
                                preferred_element_type=F32)
            blocks = _split_blocks(s, qi + 1)
            blocks[qi] = _causal_mask(blocks[qi])
            out.append(blocks)
        return out

    def attend(qi, blocks1, blocks2, shift):
        rows = slice(qi * blk, (qi + 1) * blk)
        nk = (qi + 1) * blk
        p1, l1 = _prefix_softmax(blocks1, shift)
        p2, l2 = _prefix_softmax(blocks2, shift)
        r1 = 1.0 / l1
        r2 = lam / l2
        w = jnp.concatenate([(a * r1 - b * r2).astype(BF16) for a, b in zip(p1, p2)], axis=0)
        o = jnp.dot(vt_ref[:, 0:nk], w, preferred_element_type=F32)
        ms = jnp.mean(o * o, axis=0, keepdims=True)
        on = (o * lax.rsqrt(ms + NORM_EPS)).T * gs_ref[...]
        o_ref[0, rows, :] = (on * (1.0 - lam_init)).astype(o_ref.dtype)

    def run(shift):
        for n in range(nb):
            rows = slice(n * blk, (n + 1) * blk)
            vt_ref[:, rows] = v_ref[0, rows, :].astype(F32).T.astype(BF16)

        ahead = masked_scores(0)
        for qi in range(nb):
            blocks = ahead
            if qi + 1 < nb:
                ahead = masked_scores(qi + 1)
            attend(qi, *blocks, shift)

    _run_with_score_bound(bound_ref, run)


def _diff_attention(proj, bound, lq1, lk1, lq2, lk2, gs, lam_init):
    b, s, _ = proj.shape
    qoff = 3 * MOBA_WIDTH // DIFF_V_DIM
    head = lambda off: pl.BlockSpec((1, s, DIFF_V_DIM), lambda bi, h: (bi, 0, off + h))
    full = lambda shape: pl.BlockSpec(shape, lambda bi, h: (0,) * len(shape))
    row = lambda v: v.reshape(1, -1).astype(F32)
    return pl.pallas_call(
        functools.partial(_diff_kernel, lam_init),
        grid=(b, DIFF_HEADS),
        in_specs=[pl.BlockSpec(memory_space=pltpu.SMEM),
                  head(qoff), head(qoff + DIFF_HEADS), head(qoff + 2 * DIFF_HEADS),
                  full((1, HEAD_DIM)), full((1, HEAD_DIM)), full((1, HEAD_DIM)), full((1, HEAD_DIM)),
                  full((1, DIFF_V_DIM))],
        out_specs=pl.BlockSpec((1, s, DIFF_V_DIM), lambda bi, h: (bi, 0, h)),
        out_shape=jax.ShapeDtypeStruct((b, s, DIFF_HEADS * DIFF_V_DIM), BF16),
        scratch_shapes=[pltpu.VMEM((DIFF_V_DIM, s), BF16)],
        compiler_params=_params(2),
        name="diff_attention",
    )(bound, proj, proj, proj, row(lq1), row(lk1), row(lq2), row(lk2), row(gs))


def _mem_kv_kernel(m_ref, wk_ref, wv_ref, gm_ref, gk_ref, k_ref, v_ref):
    x = _row_rmsnorm(m_ref[...], gm_ref[...]).astype(BF16)
    k = jnp.dot(x, wk_ref[...].astype(BF16), preferred_element_type=F32)
    for h in range(MEM_HEADS):
        cols = slice(h * HEAD_DIM, (h + 1) * HEAD_DIM)
        k_ref[:, cols] = _row_rmsnorm(k[:, cols], gk_ref[...]).astype(k_ref.dtype)
    v_ref[...] = jnp.dot(x, wv_ref[...].astype(BF16), preferred_element_type=F32).astype(v_ref.dtype)


def _mem_kv(mem, wk, wv, g_mem, gk, tm):
    m, d = mem.shape
    row = lambda g: g.reshape(1, -1).astype(F32)
    return pl.pallas_call(
        _mem_kv_kernel,
        grid=(m // tm,),
        in_specs=[pl.BlockSpec((tm, d), lambda i: (i, 0)),
                  pl.BlockSpec((d, MEM_WIDTH), lambda i: (0, 0)),
                  pl.BlockSpec((d, MEM_WIDTH), lambda i: (0, 0)),
                  pl.BlockSpec((1, d), lambda i: (0, 0)),
                  pl.BlockSpec((1, HEAD_DIM), lambda i: (0, 0))],
        out_specs=[pl.BlockSpec((tm, MEM_WIDTH), lambda i: (i, 0)),
                   pl.BlockSpec((tm, MEM_WIDTH), lambda i: (i, 0))],
        out_shape=[jax.ShapeDtypeStruct((m, MEM_WIDTH), BF16), jax.ShapeDtypeStruct((m, MEM_WIDTH), BF16)],
        compiler_params=_params(1),
        name="mem_kv",
    )(mem, wk, wv, row(g_mem), row(gk))


def _cross_kernel(h_ref, k_ref, v_ref, wq_ref, wo_ref, gq_ref, gin_ref, gout_ref, o_ref, on_ref):
    hn = _row_rmsnorm(h_ref[...], gin_ref[...]).astype(BF16)
    q = jnp.dot(hn, wq_ref[...], preferred_element_type=F32)
    heads = []
    for h in range(MEM_HEADS):
        cols = slice(h * HEAD_DIM, (h + 1) * HEAD_DIM)
        qh = _row_rmsnorm(q[:, cols], gq_ref[...]).astype(BF16)
        s = lax.dot_general(qh, k_ref[0, :, cols], NT_DIMS, preferred_element_type=F32) * ATTN_SCALE
        e = jnp.exp(s - jnp.max(s, axis=-1, keepdims=True))
        p = (e / jnp.sum(e, axis=-1, keepdims=True)).astype(BF16)
        heads.append(jnp.dot(p, v_ref[0, :, cols], preferred_element_type=F32).astype(BF16))
    o = jnp.concatenate(heads, axis=-1)
    out = h_ref[...] + jnp.dot(o, wo_ref[...], preferred_element_type=F32)
    o_ref[...] = out
    on_ref[...] = _row_rmsnorm(out, gout_ref[...]).astype(on_ref.dtype)


def _cross_attention(h, k, v, wq, wo, gq, g_in, g_out, seq, tm):
    m, d = h.shape
    mem_len = k.shape[1]
    per_batch = seq // tm
    row = lambda g: g.reshape(1, -1).astype(F32)
    return pl.pallas_call(
        _cross_kernel,
        grid=(m // tm,),
        in_specs=[pl.BlockSpec((tm, d), lambda i: (i, 0)),
                  pl.BlockSpec((1, mem_len, MEM_WIDTH), lambda i: (i // per_batch, 0, 0)),
                  pl.BlockSpec((1, mem_len, MEM_WIDTH), lambda i: (i // per_batch, 0, 0)),
                  pl.BlockSpec((d, MEM_WIDTH), lambda i: (0, 0), pipeline_mode=pl.Buffered(1)),
                  pl.BlockSpec((MEM_WIDTH, d), lambda i: (0, 0), pipeline_mode=pl.Buffered(1)),
                  pl.BlockSpec((1, HEAD_DIM), lambda i: (0, 0)),
                  pl.BlockSpec((1, d), lambda i: (0, 0)),
                  pl.BlockSpec((1, d), lambda i: (0, 0))],
        out_specs=[pl.BlockSpec((tm, d), lambda i: (i, 0)), pl.BlockSpec((tm, d), lambda i: (i, 0))],
        out_shape=[jax.ShapeDtypeStruct((m, d), F32), jax.ShapeDtypeStruct((m, d), BF16)],
        compiler_params=_params(1),
        name="cross_attention",
    )(h, k, v, wq, wo, row(gq), row(g_in), row(g_out))


def _layer(h, mem, l, p):
    b, s, d = h.shape
    m = b * s
    lam_init = 0.8 - 0.6 * math.exp(-0.3 * l)
    bf = lambda w: w.astype(BF16)
    h2d = h.reshape(m, d)

    proj, w_out_bf = _in_projection(h2d, p["norm_mix_g"][l], p["w_in"][l], _head_gains(p, l), _rope_tables(s),
                                    p["w_out"][l], s, *IN_PROJ_TILE)
    proj = proj.reshape(b, s, IN_WIDTH)
    out_a = _moba_attention(proj, _score_bound(p["q_norm_a"][l], p["k_norm_a"][l]))
    out_b = _diff_attention(proj, _score_bound(p["q_norm_b"][l], p["k_norm_b"][l]),
                            p["lam_q1"][l], p["lam_k1"][l], p["lam_q2"][l], p["lam_k2"][l],
                            p["diff_subln_g"][l], lam_init)
    h2d = _out_projection(out_a.reshape(m, -1), out_b.reshape(m, -1), w_out_bf, h2d, *OUT_PROJ_TILE)

    mem_len = mem.shape[1]
    mk, mv = _mem_kv(mem.reshape(b * mem_len, d), p["w_mk"][l], p["w_mv"][l], p["norm_mem_g"][l],
                     p["k_norm_m"][l], MEM_KV_TILE)
    h2d, hn = _cross_attention(h2d, mk.reshape(b, mem_len, -1), mv.reshape(b, mem_len, -1),
                               bf(p["w_mq"][l]), bf(p["w_mo"][l]), p["q_norm_m"][l],
                               p["norm_cross_g"][l], p["norm_ffn_g"][l], s, CROSS_TILE)

    act, w_down_bf = _swiglu_up(hn, p["w_gate"][l], p["w_up"][l], p["w_down"][l], *SWIGLU_TILE)
    h2d = _matmul_residual(act, w_down_bf, h2d, *DOWN_PROJ_TILE)
    return h2d.reshape(b, s, d)


def kernel(x, mem, norm_mix_g, w_in, q_norm_a, k_norm_a, q_norm_b, k_norm_b, lam_q1, lam_k1, lam_q2, lam_k2, diff_subln_g, w_out, norm_cross_g, norm_mem_g, w_mq, w_mk, w_mv, w_mo, q_norm_m, k_norm_m, norm_ffn_g, w_gate, w_up, w_down):
    p = dict(norm_mix_g=norm_mix_g, w_in=w_in, q_norm_a=q_norm_a, k_norm_a=k_norm_a, q_norm_b=q_norm_b,
             k_norm_b=k_norm_b, lam_q1=lam_q1, lam_k1=lam_k1, lam_q2=lam_q2, lam_k2=lam_k2,
             diff_subln_g=diff_subln_g, w_out=w_out, norm_cross_g=norm_cross_g, norm_mem_g=norm_mem_g,
             w_mq=w_mq, w_mk=w_mk, w_mv=w_mv, w_mo=w_mo, q_norm_m=q_norm_m, k_norm_m=k_norm_m,
             norm_ffn_g=norm_ffn_g, w_gate=w_gate, w_up=w_up, w_down=w_down)
    h = x
    for l in range(w_in.shape[0]):
        h = _layer(h, mem, l, p)
    return h
```
